```python
import functools
import jax, jax.numpy as jnp
from jax import lax
import numpy as np

D_MODEL = 1024
BATCH = 8
SEQ = 2048
DEPTH = 1
DEC_BATCH = 128
DEC_SEQ = 1
PAST_LEN = 8192
PAGE_SIZE = 128

HEAD_DIM = 64
MIX_WIDTH = D_MODEL
RET_WIDTH = MIX_WIDTH // 2
SB_WIDTH = MIX_WIDTH - RET_WIDTH
RET_HEADS = RET_WIDTH // HEAD_DIM
SB_HEADS = SB_WIDTH // HEAD_DIM
IN_COLS = 4 * RET_WIDTH + 3 * SB_WIDTH
RET_CHUNK = 128
SB_BLOCK = 128
SB_BIAS_INIT = -6.0
ROPE_BASE = 10000.0
N_EXPERTS = 32
TOP_K = 4
D_EXPERT = D_MODEL
SWIGLU_LIMIT = 7.0
SWIGLU_ALPHA = 1.702
MOE_BLOCK = 128
NORM_EPS = 1e-6

kernel_name = "hymba_retention_stickbreaking_moe_step"

F32 = jnp.float32


def rmsnorm(x, g):
    x32 = x.astype(F32)
    y = x32 * lax.rsqrt(jnp.mean(x32 * x32, axis=-1, keepdims=True) + NORM_EPS)
    return (y * g.astype(F32)).astype(x.dtype)


def modulate(h, shift, scale):
    return h * (1 + scale[:, None, :]) + shift[:, None, :]


def rope(x, pos0):
    L = x.shape[1]
    half = HEAD_DIM // 2
    freq = ROPE_BASE ** (-jnp.arange(half, dtype=F32) / half)
    pos = pos0 + jnp.arange(L, dtype=F32)
    ang = pos[:, None] * freq[None, :]
    cos = jnp.cos(ang)[None, :, None, :]
    sin = jnp.sin(ang)[None, :, None, :]
    x32 = x.astype(F32)
    x1, x2 = x32[..., :half], x32[..., half:]
    return jnp.concatenate([x1 * cos - x2 * sin, x1 * sin + x2 * cos], axis=-1)


def retention(q, k, v, s0):
    B, L = q.shape[:2]
    C = RET_CHUNK if L % RET_CHUNK == 0 else L
    n = L // C
    lg = jnp.log1p(-jnp.exp2(-5.0 - jnp.arange(RET_HEADS, dtype=F32)))
    t = jnp.arange(C, dtype=F32)
    diff = t[:, None] - t[None, :]
    intra = jnp.exp(jnp.where(diff[None] >= 0, diff[None] * lg[:, None, None], -jnp.inf))
    q_dec = jnp.exp((t[:, None] + 1.0) * lg[None, :])
    k_dec = jnp.exp((C - 1.0 - t[:, None]) * lg[None, :])
    chunk_dec = jnp.exp(C * lg)

    def to_chunks(a):
        return a.reshape(B, n, C, RET_HEADS, HEAD_DIM).transpose(1, 0, 2, 3, 4)

    def body(S, qkv):
        qc, kc, vc = qkv
        scores = jnp.einsum('bthd,bshd->bhts', qc, kc) * intra[None]
        o = (jnp.einsum('bhts,bshe->bthe', scores, vc)
             + jnp.einsum('bthd,bhde->bthe', qc * q_dec[None, :, :, None], S))
        S = (S * chunk_dec[None, :, None, None]
             + jnp.einsum('bshd,bshe->bhde', kc * k_dec[None, :, :, None], vc))
        return S, o

    S, o = lax.scan(body, s0.astype(F32), (to_chunks(q), to_chunks(k), to_chunks(v)))
    o = o.transpose(1, 0, 2, 3, 4).reshape(B, L, RET_HEADS, HEAD_DIM)
    return o, S


def head_group_norm(o):
    mu = jnp.mean(o, axis=-1, keepdims=True)
    d = o - mu
    return d * lax.rsqrt(jnp.mean(d * d, axis=-1, keepdims=True) + NORM_EPS)


def sb_key_block(q, q_pos, kb, vb, k_pos, bias, acc):
    z = (jnp.einsum('bqhd,bkhd->bhqk', q, kb.astype(F32)) * (HEAD_DIM ** -0.5)
         + bias.astype(F32)[None, :, None, None])
    valid = (k_pos[None, :] < q_pos[:, None])[None, None]
    lneg = jnp.where(valid, jax.nn.log_sigmoid(-z), 0.0)
    rev = lax.cumsum(lneg, axis=3, reverse=True)
    log_w = jax.nn.log_sigmoid(z) + (rev - lneg) + acc[..., None]
    w = jnp.where(valid, jnp.exp(log_w), 0.0)
    o = jnp.einsum('bhqk,bkhd->bqhd', w, vb.astype(F32))
    return o, acc + rev[..., 0]


def sb_prompt(q, k, v, bias):
    B, L = q.shape[:2]
    QB = SB_BLOCK if L % SB_BLOCK == 0 else L
    k_pos = jnp.arange(L)
    acc0 = jnp.zeros((B, SB_HEADS, QB), F32)

    def one_block(b):
        qb = lax.dynamic_slice_in_dim(q, b * QB, QB, axis=1)
        q_pos = b * QB + jnp.arange(QB)
        o, _ = sb_key_block(qb, q_pos, k, v, k_pos, bias, acc0)
        return o

    o = lax.map(one_block, jnp.arange(L // QB))
    return o.transpose(1, 0, 2, 3, 4).reshape(B, L, SB_HEADS, HEAD_DIM)


def sb_decode(q, k, v, bias, cache_k, cache_v, page_table):
    B, L = q.shape[:2]
    n_pages = PAST_LEN // PAGE_SIZE
    q_pos = PAST_LEN + jnp.arange(L)
    o, acc = sb_key_block(q, q_pos, k, v, q_pos, bias, jnp.zeros((B, SB_HEADS, L), F32))

    def page_step(carry, p):
        o, acc = carry
        phys = page_table[:, p]
        k_pos = p * PAGE_SIZE + jnp.arange(PAGE_SIZE)
        ob, acc = sb_key_block(q, q_pos, cache_k[phys], cache_v[phys], k_pos, bias, acc)
        return (o + ob, acc), None

    (o, _), _ = lax.scan(page_step, (o, acc), jnp.arange(n_pages - 1, -1, -1))
    return o


def swiglu_expert(xb, w_gu, b_gu, w_dn, b_dn):
    h = xb @ w_gu + b_gu
    g, u = h[:, :D_EXPERT], h[:, D_EXPERT:]
    g = jnp.minimum(g, SWIGLU_LIMIT)
    u = jnp.clip(u, -SWIGLU_LIMIT, SWIGLU_LIMIT)
    a = g * jax.nn.sigmoid(SWIGLU_ALPHA * g) * (u + 1)
    return a @ w_dn + b_dn


def moe(x, w_router, b_router, w_gu, b_gu, w_dn, b_dn):
    B, L, D = x.shape
    T = B * L
    TK = T * TOP_K
    xt = x.reshape(T, D)
    logits = (xt @ w_router + b_router).astype(F32)
    top_val, top_idx = lax.top_k(logits, TOP_K)
    gates = jax.nn.softmax(top_val, axis=-1)
    flat_e = top_idx.reshape(-1)
    flat_tok = jnp.repeat(jnp.arange(T, dtype=jnp.int32), TOP_K)
    order = jnp.argsort(flat_e)
    sorted_e = flat_e[order]
    sorted_tok = flat_tok[order]
    sorted_gate = gates.reshape(-1)[order]
    counts = jnp.zeros((N_EXPERTS,), jnp.int32).at[flat_e].add(1)
    padded = (counts + MOE_BLOCK - 1) // MOE_BLOCK * MOE_BLOCK
    pad_end = jnp.cumsum(padded)
    pad_start = pad_end - padded
    start = jnp.cumsum(counts) - counts
    dest = pad_start[sorted_e] + (jnp.arange(TK, dtype=jnp.int32) - start[sorted_e])
    n_blocks = -(-TK // MOE_BLOCK) + N_EXPERTS
    n_rows = n_blocks * MOE_BLOCK
    row_tok = jnp.full((n_rows,), T, jnp.int32).at[dest].set(sorted_tok)
    block_e = jnp.minimum(
        jnp.searchsorted(pad_end, jnp.arange(n_blocks, dtype=jnp.int32) * MOE_BLOCK, side='right'),
        N_EXPERTS - 1)
    x_pad = jnp.concatenate([xt, jnp.zeros((1, D), xt.dtype)], axis=0)
    xb = x_pad[row_tok].reshape(n_blocks, MOE_BLOCK, D)

    def run_block(args):
        xblk, e = args
        return swiglu_expert(xblk, w_gu[e], b_gu[e], w_dn[e], b_dn[e])

    yb = lax.map(run_block, (xb, block_e)).reshape(n_rows, D)
    contrib = yb[dest] * sorted_gate[:, None].astype(yb.dtype)
    return jax.ops.segment_sum(contrib, sorted_tok, num_segments=T).reshape(B, L, D)


def trunk_layer(x, c, pos0, s0, sb_attend, w_ada, b_ada, g_mix, g_ffn, w_in, sb_bias, ret_gn, w_out,
                w_router, b_router, w_gu, b_gu, w_dn, b_dn):
    B, L, _ = x.shape
    mod = jax.nn.silu(c) @ w_ada + b_ada
    sh1, sc1, gt1, sh2, sc2, gt2 = jnp.split(mod, 6, axis=-1)
    h = modulate(rmsnorm(x, g_mix), sh1, sc1)
    proj = h @ w_in
    rq, rk, rv, rg, sq, sk, sv = jnp.split(
        proj, [RET_WIDTH, 2 * RET_WIDTH, 3 * RET_WIDTH, 4 * RET_WIDTH,
               4 * RET_WIDTH + SB_WIDTH, 4 * RET_WIDTH + 2 * SB_WIDTH], axis=-1)
    q_r = rope(rq.reshape(B, L, RET_HEADS, HEAD_DIM), pos0)
    k_r = rope(rk.reshape(B, L, RET_HEADS, HEAD_DIM), pos0) * (HEAD_DIM ** -0.5)
    v_r = rv.reshape(B, L, RET_HEADS, HEAD_DIM).astype(F32)
    o_r, s_new = retention(q_r, k_r, v_r, s0)
    o_r = (head_group_norm(o_r).reshape(B, L, RET_WIDTH) * ret_gn.astype(F32)
           * jax.nn.silu(rg.astype(F32)))
    k_s = sk.reshape(B, L, SB_HEADS, HEAD_DIM)
    v_s = sv.reshape(B, L, SB_HEADS, HEAD_DIM)
    o_s = sb_attend(sq.reshape(B, L, SB_HEADS, HEAD_DIM).astype(F32), k_s, v_s, sb_bias)
    o_s = o_s.reshape(B, L, SB_WIDTH)
    mixed = jnp.concatenate([o_r, o_s], axis=-1).astype(x.dtype) @ w_out
    x = x + gt1[:, None, :] * mixed
    h2 = modulate(rmsnorm(x, g_ffn), sh2, sc2)
    x = x + gt2[:, None, :] * moe(h2, w_router, b_router, w_gu, b_gu, w_dn, b_dn)
    return x, k_s, v_s, s_new.astype(s0.dtype)


def final_norm(x, c, w_ada_final, b_ada_final, g_final):
    shift, scale = jnp.split(jax.nn.silu(c) @ w_ada_final + b_ada_final, 2, axis=-1)
    return modulate(rmsnorm(x, g_final), shift, scale)


def setup_inputs(seed: int = 0) -> dict:
    key = jax.random.key(seed)
    ks = jax.random.split(key, 26)
    n_pages = PAST_LEN // PAGE_SIZE
    n_phys = (DEC_BATCH * n_pages * 5) // 4

    def nrm(k, shape, scale):
        return jax.random.normal(k, shape, F32) * scale

    page_table = jax.random.permutation(ks[5], n_phys)[:DEC_BATCH * n_pages]
    page_table = page_table.reshape(DEC_BATCH, n_pages).astype(jnp.int32)
    d_in = D_MODEL ** -0.5
    return {
        "x_prompt": nrm(ks[0], (BATCH, SEQ, D_MODEL), 1.0),
        "x_sample": nrm(ks[1], (DEC_BATCH, DEC_SEQ, D_MODEL), 1.0),
        "cache_k": nrm(ks[2], (DEPTH, n_phys, PAGE_SIZE, SB_HEADS, HEAD_DIM), 1.0),
        "cache_v": nrm(ks[3], (DEPTH, n_phys, PAGE_SIZE, SB_HEADS, HEAD_DIM), 1.0),
        "state_ret": nrm(ks[4], (DEPTH, DEC_BATCH, RET_HEADS, HEAD_DIM, HEAD_DIM), 0.5),
        "page_table": page_table,
        "c_prompt": nrm(ks[6], (BATCH, D_MODEL), 1.0),
        "c_sample": nrm(ks[7], (DEC_BATCH, D_MODEL), 1.0),
        "w_ada": nrm(ks[8], (DEPTH, D_MODEL, 6 * D_MODEL), 0.5 * d_in),
        "b_ada": nrm(ks[9], (DEPTH, 6 * D_MODEL), 0.02),
        "g_mix": 1.0 + nrm(ks[10], (DEPTH, D_MODEL), 0.02),
        "g_ffn": 1.0 + nrm(ks[11], (DEPTH, D_MODEL), 0.02),
        "w_in": nrm(ks[12], (DEPTH, D_MODEL, IN_COLS), d_in),
        "sb_bias": SB_BIAS_INIT + nrm(ks[24], (DEPTH, SB_HEADS), 0.1),
        "ret_gn": 1.0 + nrm(ks[13], (DEPTH, RET_WIDTH), 0.02),
        "w_out": nrm(ks[14], (DEPTH, MIX_WIDTH, D_MODEL), MIX_WIDTH ** -0.5),
        "w_router": nrm(ks[15], (DEPTH, D_MODEL, N_EXPERTS), d_in),
        "b_router": nrm(ks[16], (DEPTH, N_EXPERTS), 0.01),
        "w_gate_up": nrm(ks[17], (DEPTH, N_EXPERTS, D_MODEL, 2 * D_EXPERT), d_in),
        "b_gate_up": nrm(ks[18], (DEPTH, N_EXPERTS, 2 * D_EXPERT), 0.02),
        "w_down": nrm(ks[19], (DEPTH, N_EXPERTS, D_EXPERT, D_MODEL), D_EXPERT ** -0.5),
        "b_down": nrm(ks[20], (DEPTH, N_EXPERTS, D_MODEL), 0.02),
        "w_ada_final": nrm(ks[21], (D_MODEL, 2 * D_MODEL), 0.5 * d_in),
        "b_ada_final": nrm(ks[22], (2 * D_MODEL,), 0.02),
        "g_final": 1.0 + nrm(ks[23], (D_MODEL,), 0.02),
    }


def reference(x_prompt, x_sample, cache_k, cache_v, state_ret, page_table, c_prompt, c_sample,
              w_ada, b_ada, g_mix, g_ffn, w_in, sb_bias, ret_gn, w_out, w_router, b_router,
              w_gate_up, b_gate_up, w_down, b_down, w_ada_final, b_ada_final, g_final):
    xp, xs = x_prompt, x_sample
    kp, vp, sp, ksm, vsm, ssm = [], [], [], [], [], []
    for l in range(DEPTH):
        weights = (w_ada[l], b_ada[l], g_mix[l], g_ffn[l], w_in[l], sb_bias[l], ret_gn[l], w_out[l],
                   w_router[l], b_router[l], w_gate_up[l], b_gate_up[l], w_down[l], b_down[l])
        s0 = jnp.zeros((xp.shape[0], RET_HEADS, HEAD_DIM, HEAD_DIM), state_ret.dtype)
        xp, k1, v1, s1 = trunk_layer(xp, c_prompt, 0, s0, sb_prompt, *weights)
        decode_attend = functools.partial(sb_decode, cache_k=cache_k[l], cache_v=cache_v[l],
                                          page_table=page_table)
        xs, k2, v2, s2 = trunk_layer(xs, c_sample, PAST_LEN, state_ret[l], decode_attend, *weights)
        kp.append(k1); vp.append(v1); sp.append(s1)
        ksm.append(k2); vsm.append(v2); ssm.append(s2)
    y_prompt = final_norm(xp, c_prompt, w_ada_final, b_ada_final, g_final)
    y_sample = final_norm(xs, c_sample, w_ada_final, b_ada_final, g_final)
    return (y_prompt, y_sample, jnp.stack(kp), jnp.stack(vp), jnp.stack(sp),
            jnp.stack(ksm), jnp.stack(vsm), jnp.stack(ssm))
```

```python
import functools

import jax
import jax.numpy as jnp
from jax import lax
from jax.experimental import pallas as pl
from jax.experimental.pallas import tpu as pltpu

F32, BF16, I32 = jnp.float32, jnp.bfloat16, jnp.int32

HEAD_DIM = 64
RET_CHUNK = 128
SB_BLOCK = 128
ROPE_BASE = 10000.0
TOP_K = 4
SWIGLU_LIMIT = 7.0
SWIGLU_ALPHA = 1.702
NORM_EPS = 1e-6

V7X_LANES = 128
V7X_VMEM_BYTES = 64 * 2**20
VMEM_LIMIT_BYTES = V7X_VMEM_BYTES - 8 * 2**20

ROW_TILE = 512
MOE_ROWS = 256
DEC_PAGES = 16
N_ROW_GROUPS = 5


def _cparams(*sem):
    return pltpu.CompilerParams(dimension_semantics=sem, vmem_limit_bytes=VMEM_LIMIT_BYTES)


def _dot(a, b):
    return jnp.dot(a, b, preferred_element_type=F32)


def _dot_nt(a, b):
    return lax.dot_general(a, b, (((1,), (1,)), ((), ())), preferred_element_type=F32)


def _dot_tn(a, b):
    return lax.dot_general(a, b, (((0,), (0,)), ((), ())), preferred_element_type=F32)


def _split_bf16(x):
    hi = x.astype(BF16)
    return hi, (x - hi.astype(F32)).astype(BF16)


def _silu(x):
    return x * jax.nn.sigmoid(x)


def _rms_mod(x, g, shift, scale):
    y = x * lax.rsqrt(jnp.mean(x * x, axis=-1, keepdims=True) + NORM_EPS) * g
    return y * (1 + scale) + shift


def _group_norm(o, axis):
    mu = jnp.mean(o, axis=axis, keepdims=True)
    d = o - mu
    return d * lax.rsqrt(jnp.mean(d * d, axis=axis, keepdims=True) + NORM_EPS)


def _stick_terms(z):
    e = jnp.exp(-jnp.abs(z))
    r = 1.0 / (1.0 + e)
    beta = jnp.where(z >= 0, r, e * r)
    lneg = -(jnp.maximum(z, 0.0) + jnp.log1p(e))
    return beta, lneg


def _suffix_matrix(n):
    j = jnp.arange(n)
    return jnp.concatenate([(j[:, None] >= j[None, :]), jnp.ones((n, n), bool)], axis=1).astype(BF16)


def _head_of_lane_mask(heads):
    shape = (heads, heads * HEAD_DIM)
    return lax.broadcasted_iota(I32, shape, 0) == lax.broadcasted_iota(I32, shape, 1) // HEAD_DIM


def _ada_kernel(c_ref, w_ref, b_ref, o_ref):
    a = _silu(c_ref[...]).astype(BF16)
    o_ref[...] = _dot(a, w_ref[...].astype(BF16)) + b_ref[...]


def _ada(c, w, b, tn=512):
    m, d = c.shape
    n = w.shape[1]
    return pl.pallas_call(
        _ada_kernel,
        out_shape=jax.ShapeDtypeStruct((m, n), F32),
        grid=(n // tn,),
        in_specs=[
            pl.BlockSpec((m, d), lambda j: (0, 0)),
            pl.BlockSpec((d, tn), lambda j: (0, j)),
            pl.BlockSpec((1, tn), lambda j: (0, j)),
        ],
        out_specs=pl.BlockSpec((m, tn), lambda j: (0, j)),
        compiler_params=_cparams("arbitrary"),
        name="ada",
    )(c, w, b.reshape(1, n))


def _swap_halves(x):
    lane = lax.broadcasted_iota(I32, x.shape, 1)
    fwd = pltpu.roll(x, V7X_LANES - HEAD_DIM // 2, 1)
    bwd = pltpu.roll(x, HEAD_DIM // 2, 1)
    return jnp.where(lane % HEAD_DIM < HEAD_DIM // 2, fwd, bwd)


def _proj_kernel(x_ref, sh_ref, sc_ref, g_ref, w_ref, wt_ref, cos_ref, sin_ref, *out_refs, width):
    hb = _rms_mod(x_ref[...], g_ref[...], sh_ref[...], sc_ref[...]).astype(BF16)
    cos, sin = cos_ref[...], sin_ref[...]
    for gi, o_ref in enumerate(out_refs[:N_ROW_GROUPS]):
        acc = _dot(hb, w_ref[:, gi * width:(gi + 1) * width])
        if gi < 2:
            for c in range(width // V7X_LANES):
                sl = slice(c * V7X_LANES, (c + 1) * V7X_LANES)
                t = acc[:, sl]
                t = t * cos + _swap_halves(t) * sin
                o_ref[:, sl] = t * (HEAD_DIM ** -0.5) if gi == 1 else t
        else:
            o_ref[...] = acc
    for gi, o_ref in enumerate(out_refs[N_ROW_GROUPS:]):
        o_ref[...] = _dot_nt(wt_ref[gi * width:(gi + 1) * width, :], hb)


def _proj(x, shift, scale, g, w_bf16, cos, sin, *, tm, rows_per_mod):
    t_rows, d = x.shape
    width = w_bf16.shape[1] // 7
    w_row = w_bf16[:, :N_ROW_GROUPS * width]
    w_t = w_bf16[:, N_ROW_GROUPS * width:].T
    if rows_per_mod is None:
        tps, rows = 1, t_rows
        mod_spec = pl.BlockSpec((tm, d), lambda i: (i, 0))
        rope_spec = pl.BlockSpec((1, V7X_LANES), lambda i: (0, 0))
    else:
        tps, rows = rows_per_mod // tm, rows_per_mod
        mod_spec = pl.BlockSpec((None, 1, d), lambda i: (i // tps, 0, 0))
        rope_spec = pl.BlockSpec((tm, V7X_LANES), lambda i: (i % tps, 0))
    row_sds = jax.ShapeDtypeStruct((t_rows, width), F32)
    t_sds = jax.ShapeDtypeStruct((t_rows // rows, width, rows), F32)
    return pl.pallas_call(
        functools.partial(_proj_kernel, width=width),
        out_shape=(row_sds,) * N_ROW_GROUPS + (t_sds,) * 2,
        grid=(t_rows // tm,),
        in_specs=[
            pl.BlockSpec((tm, d), lambda i: (i, 0)),
            mod_spec,
            mod_spec,
            pl.BlockSpec((1, d), lambda i: (0, 0)),
            pl.BlockSpec(w_row.shape, lambda i: (0, 0)),
            pl.BlockSpec(w_t.shape, lambda i: (0, 0)),
            rope_spec,
            rope_spec,
        ],
        out_specs=(pl.BlockSpec((tm, width), lambda i: (i, 0)),) * N_ROW_GROUPS
        + (pl.BlockSpec((None, width, tm), lambda i: (i // tps, 0, i % tps)),) * 2,
        compiler_params=_cparams("arbitrary"),
        name="proj",
    )(x, shift, scale, g, w_row, w_t, cos, sin)


def _rope_tables(pos):
    half = HEAD_DIM // 2
    freq = ROPE_BASE ** (-jnp.arange(half, dtype=F32) / half)
    ang = pos[:, None] * freq[None, :]
    cos, sin = jnp.cos(ang), jnp.sin(ang)
    reps = V7X_LANES // HEAD_DIM
    return jnp.tile(cos, (1, 2 * reps)), jnp.tile(jnp.concatenate([-sin, sin], axis=1), (1, reps))


def _log_gamma(heads):
    return jnp.log1p(-jnp.exp2(-5.0 - jnp.arange(heads, dtype=F32)))


def _ret_kernel(cd_ref, q_ref, k_ref, v_ref, g_ref, gn_ref, intra_ref, qd_ref, kd_ref, o_ref, s_ref, *, heads):
    @pl.when(pl.program_id(1) == 0)
    def _():
        s_ref[...] = jnp.zeros_like(s_ref)

    q, k, v = q_ref[...], k_ref[...], v_ref[...]
    qd = q * qd_ref[...]
    kd = k * kd_ref[...]
    outs = []
    for h in range(heads):
        sl = slice(h * HEAD_DIM, (h + 1) * HEAD_DIM)
        qh, kh, vh = q[:, sl].astype(BF16), k[:, sl].astype(BF16), v[:, sl].astype(BF16)
        s_old = s_ref[h]
        scores = _dot_nt(qh, kh) * intra_ref[h]
        o = _dot(scores.astype(BF16), vh) + _dot(qd[:, sl].astype(BF16), s_old.astype(BF16))
        s_ref[h] = s_old * cd_ref[h] + _dot_tn(kd[:, sl].astype(BF16), vh)
        outs.append(_group_norm(o, -1))
    o_ref[...] = jnp.concatenate(outs, axis=1) * gn_ref[...] * _silu(g_ref[...])


def _retention_prompt(q, k, v, g, ret_gn, batch, seq):
    width = q.shape[1]
    heads = width // HEAD_DIM
    c = RET_CHUNK
    assert seq % c == 0
    nc = seq // c
    lg = _log_gamma(heads)
    t = jnp.arange(c, dtype=F32)
    diff = t[:, None] - t[None, :]
    intra = jnp.exp(jnp.where(diff[None] >= 0, diff[None] * lg[:, None, None], -jnp.inf))
    q_dec = jnp.repeat(jnp.exp((t[:, None] + 1.0) * lg[None, :]), HEAD_DIM, axis=1)
    k_dec = jnp.repeat(jnp.exp((c - 1.0 - t[:, None]) * lg[None, :]), HEAD_DIM, axis=1)
    chunk_dec = jnp.exp(c * lg)
    row = pl.BlockSpec((c, width), lambda b, i: (b * nc + i, 0))
    const2 = pl.BlockSpec((c, width), lambda b, i: (0, 0))
    return pl.pallas_call(
        functools.partial(_ret_kernel, heads=heads),
        out_shape=(
            jax.ShapeDtypeStruct((batch * seq, width), F32),
            jax.ShapeDtypeStruct((batch, heads, HEAD_DIM, HEAD_DIM), F32),
        ),
        grid=(batch, nc),
        in_specs=[
            pl.BlockSpec(memory_space=pltpu.SMEM),
            row, row, row, row,
            pl.BlockSpec((1, width), lambda b, i: (0, 0)),
            pl.BlockSpec((heads, c, c), lambda b, i: (0, 0, 0)),
            const2, const2,
        ],
        out_specs=(
            row,
            pl.BlockSpec((None, heads, HEAD_DIM, HEAD_DIM), lambda b, i: (b, 0, 0, 0)),
        ),
        compiler_params=_cparams("arbitrary", "arbitrary"),
        name="retention_prompt",
    )(chunk_dec, q, k, v, g, ret_gn.reshape(1, width), intra, q_dec, k_dec)


def _ret_dec_kernel(gam_ref, q_ref, k_ref, v_ref, g_ref, gn_ref, s_ref, o_ref, so_ref):
    gamma = gam_ref[pl.program_id(0)]
    q, k, v = q_ref[...], k_ref[...], v_ref[...]
    qg = q * gamma
    o = jnp.sum(q * k, axis=0, keepdims=True) * v
    for d in range(s_ref.shape[0]):
        s_old = s_ref[d]
        so_ref[d] = s_old * gamma + k[d:d + 1, :] * v
        o = o + qg[d:d + 1, :] * s_old
    o_ref[...] = _group_norm(o, 0) * gn_ref[...] * _silu(g_ref[...])


def _retention_decode(q_t, k_t, v_t, g_t, state_t, ret_gn):
    width, b = q_t.shape
    hd = HEAD_DIM
    heads = width // hd
    gamma = jnp.exp(_log_gamma(heads))
    vec = pl.BlockSpec((hd, b), lambda h: (h, 0))
    st = pl.BlockSpec((None, None, hd, hd, b), lambda h: (0, h, 0, 0, 0))
    return pl.pallas_call(
        _ret_dec_kernel,
        out_shape=(jax.ShapeDtypeStruct((width, b), F32), jax.ShapeDtypeStruct(state_t.shape, F32)),
        grid=(heads,),
        in_specs=[pl.BlockSpec(memory_space=pltpu.SMEM), vec, vec, vec, vec, vec, st],
        out_specs=(vec, st),
        compiler_params=_cparams("arbitrary"),
        name="retention_decode",
    )(gamma, q_t, k_t, v_t, g_t, jnp.broadcast_to(ret_gn[:, None], (width, b)), state_t)


def _sbp_kernel(bias_ref, q_ref, k_ref, v_ref, tt_ref, o_ref, acc_scr, o_scr, *, heads):
    blk = SB_BLOCK
    qi = pl.program_id(1)
    acc_scr[...] = jnp.zeros_like(acc_scr)
    o_scr[...] = jnp.zeros_like(o_scr)
    q = q_ref[...]
    q_heads = [q[:, h * HEAD_DIM:(h + 1) * HEAD_DIM].astype(BF16) for h in range(heads)]
    row = lax.broadcasted_iota(I32, (blk, blk), 0)
    col = lax.broadcasted_iota(I32, (blk, blk), 1)

    def body(jj, carry):
        j = qi - jj
        start = pl.multiple_of(j * blk, blk)
        kb = k_ref[:, pl.ds(start, blk)]
        vb = v_ref[:, pl.ds(start, blk)]
        valid = (j * blk + col) < (qi * blk + row)
        for h in range(heads):
            sl = slice(h * HEAD_DIM, (h + 1) * HEAD_DIM)
            z = _dot(q_heads[h], kb[sl, :].astype(BF16)) * (HEAD_DIM ** -0.5) + bias_ref[h]
            beta, lneg = _stick_terms(z)
            lneg = jnp.where(valid, lneg, 0.0)
            hi, lo = _split_bf16(lneg)
            sums = _dot(hi, tt_ref[...]) + _dot(lo, tt_ref[...])
            rev, tot = sums[:, :blk], sums[:, blk:]
            acc = acc_scr[h]
            w = jnp.where(valid, beta * jnp.exp(rev - lneg + acc), 0.0)
            o_scr[h] = o_scr[h] + _dot_nt(w.astype(BF16), vb[sl, :].astype(BF16))
            acc_scr[h] = acc + tot
        return carry

    lax.fori_loop(0, qi + 1, body, 0)
    o_ref[...] = jnp.concatenate([o_scr[h] for h in range(heads)], axis=1)


def _sb_prompt(q, k_t, v_t, bias, seq):
    batch, width, _ = k_t.shape
    heads = width // HEAD_DIM
    blk = SB_BLOCK
    assert seq % blk == 0
    nq = seq // blk
    kv = pl.BlockSpec((None, width, seq), lambda b, i: (b, 0, 0))
    return pl.pallas_call(
        functools.partial(_sbp_kernel, heads=heads),
        out_shape=jax.ShapeDtypeStruct(q.shape, F32),
        grid=(batch, nq),
        in_specs=[
            pl.BlockSpec(memory_space=pltpu.SMEM),
            pl.BlockSpec((blk, width), lambda b, i: (b * nq + i, 0)),
            kv, kv,
            pl.BlockSpec((blk, 2 * blk), lambda b, i: (0, 0)),
        ],
        out_specs=pl.BlockSpec((blk, width), lambda b, i: (b * nq + i, 0)),
        scratch_shapes=[pltpu.VMEM((heads, blk, blk), F32), pltpu.VMEM((heads, blk, HEAD_DIM), F32)],
        compiler_params=_cparams("arbitrary", "arbitrary"),
        name="sb_prompt",
    )(bias, q, k_t, v_t, _suffix_matrix(blk))


def _sbd_kernel(pt_ref, q_ref, ks_ref, vs_ref, bias_ref, tt_ref, ck_ref, cv_ref, o_ref,
                k_buf, v_buf, sem, acc_scr, o_scr, *, heads, pages, page, n_pages, past_len):
    bi, s = pl.program_id(0), pl.program_id(1)
    n_steps = pl.num_programs(1)
    step = bi * n_steps + s
    slot = lax.rem(step, 2)
    width = heads * HEAD_DIM
    scale = HEAD_DIM ** -0.5
    q_pos = past_len

    def page_copies(row, sweep, buf_slot, i):
        phys = pt_ref[row, n_pages - 1 - (sweep * pages + i)]
        return (pltpu.make_async_copy(ck_ref.at[0, phys], k_buf.at[buf_slot, i], sem.at[buf_slot]),
                pltpu.make_async_copy(cv_ref.at[0, phys], v_buf.at[buf_slot, i], sem.at[buf_slot]))

    def start_fetch(row, sweep, buf_slot):
        for i in range(pages):
            for cp in page_copies(row, sweep, buf_slot, i):
                cp.start()

    @pl.when(step == 0)
    def _():
        start_fetch(0, 0, 0)

    @pl.when(step + 1 < pl.num_programs(0) * n_steps)
    def _():
        last = s + 1 == n_steps
        start_fetch(jnp.where(last, bi + 1, bi), jnp.where(last, 0, s + 1), 1 - slot)

    bd = _head_of_lane_mask(heads)
    q_bd = jnp.where(bd, q_ref[...], 0.0)
    bias = bias_ref[...]

    @pl.when(s == 0)
    def _():
        z = jnp.sum(q_bd * ks_ref[...], axis=1, keepdims=True) * scale + bias[:heads]
        valid = jnp.full(z.shape, past_len, I32) < q_pos
        beta, lneg = _stick_terms(z)
        lneg = jnp.where(valid, lneg, 0.0)
        w = jnp.where(valid, beta, 0.0)
        o_scr[...] = w * vs_ref[...]
        acc_scr[...] = jnp.broadcast_to(lneg, acc_scr.shape)

    for i in range(pages):
        for cp in page_copies(bi, s, slot, i):
            cp.wait()

    q_b = q_bd.astype(BF16)
    z = jnp.concatenate([_dot(q_b, k_buf[slot, i].reshape(width, page).astype(BF16)) for i in range(pages)],
                        axis=0) * scale + bias
    r = lax.broadcasted_iota(I32, z.shape, 0)
    lane = lax.broadcasted_iota(I32, z.shape, 1)
    logical_page = (n_pages - 1 - s * pages) - r // heads
    mask = (logical_page * page + lane) < q_pos
    beta, lneg = _stick_terms(z)
    lneg = jnp.where(mask, lneg, 0.0)
    hi, lo = _split_bf16(lneg)
    sums = _dot(hi, tt_ref[...]) + _dot(lo, tt_ref[...])
    carry = acc_scr[...]
    carries = []
    for i in range(pages):
        carries.append(carry)
        carry = carry + sums[i * heads:(i + 1) * heads, page:]
    acc_scr[...] = carry
    rev = sums[:, :page] + jnp.concatenate(carries, axis=0)
    w = jnp.where(mask, beta * jnp.exp(rev - lneg), 0.0)
    o = o_scr[...]
    for i in range(pages):
        o = o + _dot_nt(w[i * heads:(i + 1) * heads].astype(BF16),
                        v_buf[slot, i].reshape(width, page).astype(BF16))
    o_scr[...] = o

    @pl.when(s == n_steps - 1)
    def _():
        o_ref[...] = jnp.sum(jnp.where(bd, o_scr[...], 0.0), axis=0, keepdims=True)


def _sb_decode(q, k_self, v_self, bias, cache_k_t, cache_v_t, page_table):
    b, _, width = q.shape
    _, _, heads, hd, page = cache_k_t.shape
    n_pages = page_table.shape[1]
    pages = min(DEC_PAGES, n_pages)
    assert n_pages % pages == 0 and page == V7X_LANES and heads * hd == width
    vec = pl.BlockSpec((None, 1, width), lambda bi, s, pt: (bi, 0, 0))
    page_buf = pltpu.VMEM((2, pages, heads, hd, page), F32)
    grid_spec = pltpu.PrefetchScalarGridSpec(
        num_scalar_prefetch=1,
        grid=(b, n_pages // pages),
        in_specs=[
            vec, vec, vec,
            pl.BlockSpec((pages * heads, 1), lambda bi, s, pt: (0, 0)),
            pl.BlockSpec((page, 2 * page), lambda bi, s, pt: (0, 0)),
            pl.BlockSpec(memory_space=pl.ANY),
            pl.BlockSpec(memory_space=pl.ANY),
        ],
        out_specs=vec,
        scratch_shapes=[page_buf, page_buf, pltpu.SemaphoreType.DMA((2,)),
                        pltpu.VMEM((heads, page), F32), pltpu.VMEM((heads, width), F32)],
    )
    return pl.pallas_call(
        functools.partial(_sbd_kernel, heads=heads, pages=pages, page=page, n_pages=n_pages,
                          past_len=n_pages * page),
        out_shape=jax.ShapeDtypeStruct(q.shape, F32),
        grid_spec=grid_spec,
        compiler_params=_cparams("arbitrary", "arbitrary"),
        name="sb_decode",
    )(page_table, q, k_self, v_self, jnp.tile(bias, pages).reshape(pages * heads, 1), _suffix_matrix(page),
      cache_k_t, cache_v_t)


def _out_kernel(or_ref, os_ref, x_ref, gt_ref, sh_ref, sc_ref, g_ref, w_ref, wrh_ref, wrl_ref, br_ref,
                x1_ref, h2_ref, lg_ref):
    half = or_ref.shape[1]
    mixed = _dot(or_ref[...].astype(BF16), w_ref[:half, :]) + _dot(os_ref[...].astype(BF16), w_ref[half:, :])
    x1 = x_ref[...] + gt_ref[...] * mixed
    x1_ref[...] = x1
    h2 = _rms_mod(x1, g_ref[...], sh_ref[...], sc_ref[...])
    h2_ref[...] = h2
    hh, hl = _split_bf16(h2)
    lg_ref[...] = (_dot_nt(wrh_ref[...], hh) + _dot_nt(wrh_ref[...], hl) + _dot_nt(wrl_ref[...], hh)
                   + br_ref[...])


def _out_proj(o_r, o_s, x, gate, shift, scale, g, w_bf16, w_router, b_router, *, tm, rows_per_mod):
    t_rows, d = x.shape
    half = o_r.shape[1]
    n_exp = w_router.shape[1]
    if rows_per_mod is None:
        mod_spec = pl.BlockSpec((tm, d), lambda i: (i, 0))
    else:
        tps = rows_per_mod // tm
        mod_spec = pl.BlockSpec((None, 1, d), lambda i: (i // tps, 0, 0))
    wr_hi, wr_lo = _split_bf16(w_router.T)
    const = lambda shape: pl.BlockSpec(shape, lambda i: (0, 0))
    return pl.pallas_call(
        _out_kernel,
        out_shape=(
            jax.ShapeDtypeStruct((t_rows, d), F32),
            jax.ShapeDtypeStruct((t_rows, d), F32),
            jax.ShapeDtypeStruct((n_exp, t_rows), F32),
        ),
        grid=(t_rows // tm,),
        in_specs=[
            pl.BlockSpec((tm, half), lambda i: (i, 0)),
            pl.BlockSpec((tm, half), lambda i: (i, 0)),
            pl.BlockSpec((tm, d), lambda i: (i, 0)),
            mod_spec, mod_spec, mod_spec,
            const((1, d)), const(w_bf16.shape), const((n_exp, d)), const((n_exp, d)), const((n_exp, 1)),
        ],
        out_specs=(
            pl.BlockSpec((tm, d), lambda i: (i, 0)),
            pl.BlockSpec((tm, d), lambda i: (i, 0)),
            pl.BlockSpec((n_exp, tm), lambda i: (0, i)),
        ),
        compiler_params=_cparams("arbitrary"),
        name="out_proj",
    )(o_r, o_s, x, gate, shift, scale, g, w_bf16, wr_hi, wr_lo, b_router.reshape(n_exp, 1))


def _route_kernel(lg_ref, tri_e_ref, upper_ref, dest_ref, gate_ref, be_ref, zf_ref, na_ref,
                  cnt_scr, base_scr, start_scr, *, tm):
    ph, i = pl.program_id(0), pl.program_id(1)
    logits = lg_ref[...]
    n_exp, tn = logits.shape
    e_iota = lax.broadcasted_iota(I32, (n_exp, tn), 0)
    work = logits
    sels, vals = [], []
    for _ in range(TOP_K):
        m = jnp.max(work, axis=0, keepdims=True)
        idx = jnp.min(jnp.where(work == m, e_iota, n_exp), axis=0, keepdims=True)
        sel = e_iota == idx
        sels.append(sel)
        vals.append(m)
        work = jnp.where(sel, -jnp.inf, work)
    chosen = sum(jnp.where(sel, 1.0, 0.0) for sel in sels)
    tile_count = jnp.sum(chosen, axis=1, keepdims=True)

    @pl.when(ph == 0)
    def _():
        @pl.when(i == 0)
        def _():
            cnt_scr[...] = jnp.zeros_like(cnt_scr)

        cnt_scr[...] = cnt_scr[...] + tile_count

    @pl.when((ph == 1) & (i == 0))
    def _():
        nblk = jnp.floor((cnt_scr[...] + (tm - 1)) * (1.0 / tm))
        first = _dot(tri_e_ref[...], nblk.astype(BF16))
        last = first + nblk
        start_scr[...] = first * tm
        base_scr[...] = jnp.zeros_like(base_scr)
        n_act = jnp.sum(nblk, axis=0, keepdims=True)
        na_ref[...] = n_act.astype(I32)
        for cb in range(be_ref.shape[1] // V7X_LANES):
            sl = slice(cb * V7X_LANES, (cb + 1) * V7X_LANES)
            bidx = (lax.broadcasted_iota(I32, (1, V7X_LANES), 1) + cb * V7X_LANES).astype(F32)
            owner = jnp.sum(jnp.where(last <= bidx, 1.0, 0.0), axis=0, keepdims=True)
            be_ref[:, sl] = jnp.minimum(owner, n_exp - 1.0).astype(I32)
            is_last = jnp.sum(jnp.where((last == bidx + 1.0) & (nblk > 0), 1.0, 0.0), axis=0, keepdims=True)
            zf_ref[:, sl] = jnp.where((is_last > 0) | (bidx >= n_act), 1, 0).astype(I32)

    @pl.when(ph == 1)
    def _():
        before = _dot(chosen.astype(BF16), upper_ref[...])
        row_of = start_scr[:, :1] + base_scr[:, :1] + before
        exps = [jnp.exp(v - vals[0]) for v in vals]
        denom = sum(exps)
        for kk in range(TOP_K):
            dest_ref[kk:kk + 1, :] = jnp.sum(jnp.where(sels[kk], row_of, 0.0), axis=0, keepdims=True).astype(I32)
            gate_ref[kk:kk + 1, :] = exps[kk] / denom
        base_scr[...] = base_scr[...] + tile_count


def _route(logits_t, tm, n_blocks):
    n_exp, t = logits_t.shape
    tn = max(c for c in (512, 384, 256, 128) if t % c == 0)
    nt = t // tn
    nb_pad = -(-n_blocks // V7X_LANES) * V7X_LANES
    assert t // tm + 1 < 256
    e = jnp.arange(n_exp)
    tri_e = (e[None, :] < e[:, None]).astype(BF16)
    tt = jnp.arange(tn)
    upper = (tt[:, None] < tt[None, :]).astype(BF16)
    tok = lambda ph, i: (0, i * ph)
    const = lambda ph, i: (0, 0)
    return pl.pallas_call(
        functools.partial(_route_kernel, tm=tm),
        out_shape=(
            jax.ShapeDtypeStruct((TOP_K, t), I32),
            jax.ShapeDtypeStruct((TOP_K, t), F32),
            jax.ShapeDtypeStruct((1, nb_pad), I32),
            jax.ShapeDtypeStruct((1, nb_pad), I32),
            jax.ShapeDtypeStruct((1, V7X_LANES), I32),
        ),
        grid=(2, nt),
        in_specs=[
            pl.BlockSpec((n_exp, tn), lambda ph, i: (0, i)),
            pl.BlockSpec((n_exp, n_exp), const),
            pl.BlockSpec((tn, tn), const),
        ],
        out_specs=(
            pl.BlockSpec((TOP_K, tn), tok),
            pl.BlockSpec((TOP_K, tn), tok),
            pl.BlockSpec((1, nb_pad), const),
            pl.BlockSpec((1, nb_pad), const),
            pl.BlockSpec((1, V7X_LANES), const),
        ),
        scratch_shapes=[pltpu.VMEM((n_exp, V7X_LANES), F32)] * 3,
        compiler_params=_cparams("arbitrary", "arbitrary"),
        name="route",
    )(logits_t, tri_e, upper)


def _lane_group(t):
    return lax.shift_right_logical(t, 7), lax.bitwise_and(t, V7X_LANES - 1)


def _dispatch_kernel(zf_ref, dest_ref, h_ref, xb_ref, zero_scr, sem, zsem, *, tm, n_blocks):
    td = h_ref.shape[0]

    def zero_copy(b):
        return pltpu.make_async_copy(zero_scr, xb_ref.at[pl.ds(b * tm, tm)], zsem)

    @pl.when(pl.program_id(0) == 0)
    def _():
        zero_scr[...] = jnp.zeros_like(zero_scr)

        def start(b, c):
            @pl.when(zf_ref[b] > 0)
            def _():
                zero_copy(b).start()
            return c

        def wait(b, c):
            @pl.when(zf_ref[b] > 0)
            def _():
                zero_copy(b).wait()
            return c

        lax.fori_loop(0, n_blocks, start, 0)
        lax.fori_loop(0, n_blocks, wait, 0)

    def row_copy(t, dst_row):
        return pltpu.make_async_copy(h_ref.at[pl.ds(t, 1)], xb_ref.at[pl.ds(dst_row, 1)], sem)

    def start_tok(t, c):
        grp, lane = _lane_group(t)
        for kk in range(TOP_K):
            row_copy(t, dest_ref[grp, kk, lane]).start()
        return c

    def wait_tok(t, c):
        for kk in range(TOP_K):
            row_copy(0, 0).wait()
        return c

    lax.fori_loop(0, td, start_tok, 0)
    lax.fori_loop(0, td, wait_tok, 0)


def _dispatch(h2, dest3, zero_flags, tm, n_blocks):
    t, d = h2.shape
    groups = dest3.shape[0]
    m = max(c for c in (4, 3, 2, 1) if groups % c == 0)
    td = m * V7X_LANES
    grid_spec = pltpu.PrefetchScalarGridSpec(
        num_scalar_prefetch=1,
        grid=(t // td,),
        in_specs=[
            pl.BlockSpec((m, TOP_K, V7X_LANES), lambda i, zf: (i, 0, 0), memory_space=pltpu.SMEM),
            pl.BlockSpec((td, d), lambda i, zf: (i, 0)),
        ],
        out_specs=pl.BlockSpec(memory_space=pl.ANY),
        scratch_shapes=[pltpu.VMEM((tm, d), F32), pltpu.SemaphoreType.DMA, pltpu.SemaphoreType.DMA],
    )
    return pl.pallas_call(
        functools.partial(_dispatch_kernel, tm=tm, n_blocks=n_blocks),
        out_shape=jax.ShapeDtypeStruct((n_blocks * tm, d), F32),
        grid_spec=grid_spec,
        compiler_params=_cparams("arbitrary"),
        name="dispatch",
    )(zero_flags, dest3, h2)


def _expert_kernel(be_ref, na_ref, x_ref, wgu_ref, bgu_ref, wdn_ref, bdn_ref, o_ref):
    i = pl.program_id(0)
    d_exp = wdn_ref.shape[0]

    @pl.when(i < na_ref[0])
    def _():
        h = _dot(x_ref[...].astype(BF16), wgu_ref[...].astype(BF16)) + bgu_ref[...]
        g = jnp.minimum(h[:, :d_exp], SWIGLU_LIMIT)
        u = jnp.clip(h[:, d_exp:], -SWIGLU_LIMIT, SWIGLU_LIMIT)
        a = g * jax.nn.sigmoid(SWIGLU_ALPHA * g) * (u + 1)
        o_ref[...] = _dot(a.astype(BF16), wdn_ref[...].astype(BF16)) + bdn_ref[...]

    @pl.when(i >= na_ref[0])
    def _():
        o_ref[...] = jnp.zeros_like(o_ref)


def _experts(xb, block_e, n_act, w_gu, b_gu, w_dn, b_dn, tm):
    n_rows, d = xb.shape
    _, n_exp, _, two_de = w_gu.shape
    d_exp = two_de // 2
    nb = n_rows // tm
    act = lambda i, na: jnp.minimum(i, na[0] - 1)
    grid_spec = pltpu.PrefetchScalarGridSpec(
        num_scalar_prefetch=2,
        grid=(nb,),
        in_specs=[
            pl.BlockSpec((tm, d), lambda i, be, na: (act(i, na), 0)),
            pl.BlockSpec((None, None, d, two_de), lambda i, be, na: (0, be[act(i, na)], 0, 0)),
            pl.BlockSpec((None, 1, two_de), lambda i, be, na: (be[act(i, na)], 0, 0)),
            pl.BlockSpec((None, None, d_exp, d), lambda i, be, na: (0, be[act(i, na)], 0, 0)),
            pl.BlockSpec((None, 1, d), lambda i, be, na: (be[act(i, na)], 0, 0)),
        ],
        out_specs=pl.BlockSpec((tm, d), lambda i, be, na: (i, 0)),
    )
    return pl.pallas_call(
        _expert_kernel,
        out_shape=jax.ShapeDtypeStruct((n_rows, d), F32),
        grid_spec=grid_spec,
        compiler_params=_cparams("arbitrary"),
        name="experts",
    )(block_e, n_act, xb, w_gu, b_gu.reshape(n_exp, 1, two_de), w_dn, b_dn.reshape(n_exp, 1, d))


def _final_kernel(dest_ref, x1_ref, gates_ref, gt_ref, sh_ref, sc_ref, g_ref, yb_ref, y_ref, rows_scr, sem):
    tf = x1_ref.shape[0]

    def row_copy(src_row, kk, t):
        return pltpu.make_async_copy(yb_ref.at[pl.ds(src_row, 1)], rows_scr.at[kk, pl.ds(t, 1)], sem)

    def start_tok(t, c):
        grp, lane = _lane_group(t)
        for kk in range(TOP_K):
            row_copy(dest_ref[grp, kk, lane], kk, t).start()
        return c

    def wait_tok(t, c):
        for kk in range(TOP_K):
            row_copy(0, 0, 0).wait()
        return c

    lax.fori_loop(0, tf, start_tok, 0)
    lax.fori_loop(0, tf, wait_tok, 0)
    gates = gates_ref[...]
    moe = sum(gates[:, kk:kk + 1] * rows_scr[kk] for kk in range(TOP_K))
    x2 = x1_ref[...] + gt_ref[...] * moe
    y_ref[...] = _rms_mod(x2, g_ref[...], sh_ref[...], sc_ref[...])


def _final(x1, dest3, gates_t, gate2, shift, scale, g, yb, *, tok0, tf, rows_per_mod):
    t_rows, d = x1.shape
    m = tf // V7X_LANES
    assert tok0 % tf == 0
    first = tok0 // tf
    if rows_per_mod is None:
        mod_spec = pl.BlockSpec((tf, d), lambda i: (i, 0))
    else:
        tps = rows_per_mod // tf
        mod_spec = pl.BlockSpec((None, 1, d), lambda i: (i // tps, 0, 0))
    return pl.pallas_call(
        _final_kernel,
        out_shape=jax.ShapeDtypeStruct((t_rows, d), F32),
        grid=(t_rows // tf,),
        in_specs=[
            pl.BlockSpec((m, TOP_K, V7X_LANES), lambda i: (first + i, 0, 0), memory_space=pltpu.SMEM),
            pl.BlockSpec((tf, d), lambda i: (i, 0)),
            pl.BlockSpec((tf, TOP_K), lambda i: (first + i, 0)),
            mod_spec, mod_spec, mod_spec,
            pl.BlockSpec((1, d), lambda i: (0, 0)),
            pl.BlockSpec(memory_space=pl.ANY),
        ],
        out_specs=pl.BlockSpec((tf, d), lambda i: (i, 0)),
        scratch_shapes=[pltpu.VMEM((TOP_K, tf, d), F32), pltpu.SemaphoreType.DMA],
        compiler_params=_cparams("arbitrary"),
        name="final",
    )(dest3, x1, gates_t, gate2, shift, scale, g, yb)


def kernel(x_prompt, x_sample, cache_k, cache_v, state_ret, page_table, c_prompt, c_sample, w_ada, b_ada, g_mix, g_ffn, w_in, sb_bias, ret_gn, w_out, w_router, b_router, w_gate_up, b_gate_up, w_down, b_down, w_ada_final, b_ada_final, g_final):
    batch, seq, d = x_prompt.shape
    dec_b, dec_seq, _ = x_sample.shape
    depth = w_ada.shape[0]
    assert depth == 1 and dec_seq == 1
    page = cache_k.shape[2]
    ret_w = ret_gn.shape[1]
    sb_w = w_out.shape[1] - ret_w
    assert ret_w == sb_w
    sb_h = sb_w // HEAD_DIM
    n_exp = w_router.shape[2]
    tp, ts = batch * seq, dec_b
    tm_p = min(ROW_TILE, seq)
    past_len = page_table.shape[1] * page

    c_all = jnp.concatenate([c_prompt, c_sample], axis=0)
    mod = _ada(c_all, w_ada.reshape(w_ada.shape[1:]), b_ada[0])
    fin = _ada(c_all, w_ada_final, b_ada_final)
    mods_p = [mod[:batch, i * d:(i + 1) * d].reshape(batch, 1, d) for i in range(6)]
    mods_s = [mod[batch:, i * d:(i + 1) * d] for i in range(6)]
    fin_p = [fin[:batch, i * d:(i + 1) * d].reshape(batch, 1, d) for i in range(2)]
    fin_s = [fin[batch:, i * d:(i + 1) * d] for i in range(2)]

    w_in_b = w_in[0].astype(BF16)
    w_out_b = w_out[0].astype(BF16)
    g_mix2, g_ffn2, g_fin2 = g_mix.reshape(1, d), g_ffn.reshape(1, d), g_final.reshape(1, d)

    cos_p, sin_p = _rope_tables(jnp.arange(seq, dtype=F32))
    xp = x_prompt.reshape(tp, d)
    q_r, k_r, v_r, g_r, q_s, k_st, v_st = _proj(xp, mods_p[0], mods_p[1], g_mix2, w_in_b, cos_p, sin_p,
                                                tm=tm_p, rows_per_mod=seq)
    o_r, ret_p = _retention_prompt(q_r, k_r, v_r, g_r, ret_gn[0], batch, seq)
    o_s = _sb_prompt(q_s, k_st, v_st, sb_bias[0], seq)
    x1_p, h2_p, lg_p = _out_proj(o_r, o_s, xp, mods_p[2], mods_p[3], mods_p[4], g_ffn2, w_out_b,
                                 w_router[0], b_router[0], tm=tm_p, rows_per_mod=seq)

    cos_s, sin_s = _rope_tables(past_len + jnp.arange(1, dtype=F32))
    xs = x_sample.reshape(ts, d)
    sq_r, sk_r, sv_r, sg_r, sq_s, sk_st, sv_st = _proj(xs, mods_s[0], mods_s[1], g_mix2, w_in_b, cos_s, sin_s,
                                                       tm=ts, rows_per_mod=None)
    state_t = state_ret.transpose(0, 2, 3, 4, 1)
    so_rt, ret_st = _retention_decode(sq_r.T, sk_r.T, sv_r.T, sg_r.T, state_t, ret_gn[0])
    rows3 = lambda a: a.reshape(ts, 1, sb_w)
    so_s = _sb_decode(rows3(sq_s), rows3(sk_st[0].T), rows3(sv_st[0].T), sb_bias[0],
                      cache_k.transpose(0, 1, 3, 4, 2), cache_v.transpose(0, 1, 3, 4, 2), page_table)
    x1_s, h2_s, lg_s = _out_proj(so_rt.T, so_s.reshape(ts, sb_w), xs, mods_s[2], mods_s[3], mods_s[4],
                                 g_ffn2, w_out_b, w_router[0], b_router[0], tm=ts, rows_per_mod=None)

    tm = MOE_ROWS
    t_all = tp + ts
    assert t_all % V7X_LANES == 0 and tp % V7X_LANES == 0
    n_blocks = -(-(t_all * TOP_K) // tm) + n_exp
    h2_all = jnp.concatenate([h2_p, h2_s], axis=0)
    lg_all = jnp.concatenate([lg_p, lg_s], axis=1)
    dest, gates, block_e, zero_flags, n_act = _route(lg_all, tm, n_blocks)
    dest3 = dest.reshape(TOP_K, t_all // V7X_LANES, V7X_LANES).transpose(1, 0, 2)
    gates_t = gates.T
    xb = _dispatch(h2_all, dest3, zero_flags[0, :n_blocks], tm, n_blocks)
    yb = _experts(xb, block_e[0, :n_blocks], n_act[0, :1], w_gate_up, b_gate_up[0], w_down, b_down[0], tm)

    tf_p = min(256, seq)
    y_p = _final(x1_p, dest3, gates_t, mods_p[5], fin_p[0], fin_p[1], g_fin2, yb, tok0=0, tf=tf_p,
                 rows_per_mod=seq)
    y_s = _final(x1_s, dest3, gates_t, mods_s[5], fin_s[0], fin_s[1], g_fin2, yb, tok0=tp, tf=ts,
                 rows_per_mod=None)

    kv_rows = lambda a, b, l: a.reshape(1, b, sb_h, HEAD_DIM, l).transpose(0, 1, 4, 2, 3)
    return (
        y_p.reshape(batch, seq, d),
        y_s.reshape(dec_b, 1, d),
        kv_rows(k_st, batch, seq),
        kv_rows(v_st, batch, seq),
        ret_p[None],
        kv_rows(sk_st, 1, dec_b).transpose(0, 2, 1, 3, 4),
        kv_rows(sv_st, 1, dec_b).transpose(0, 2, 1, 3, 4),
        ret_st.transpose(0, 4, 1, 2, 3),
    )
```

```python
import functools

import jax
import jax.numpy as jnp
from jax import lax
from jax.experimental import pallas as pl
from jax.experimental.pallas import tpu as pltpu

F32, BF16, I32 = jnp.float32, jnp.bfloat16, jnp.int32

HEAD_DIM = 64
RET_CHUNK = 128
SB_BLOCK = 128
ROPE_BASE = 10000.0
TOP_K = 4
SWIGLU_LIMIT = 7.0
SWIGLU_ALPHA = 1.702
NORM_EPS = 1e-6

V7X_LANES = 128
V7X_VMEM_BYTES = 64 * 2**20
VMEM_LIMIT_BYTES = V7X_VMEM_BYTES - 8 * 2**20

ROW_TILE = 512
MOE_ROWS = 256
DEC_PAGES = 16
N_ROW_GROUPS = 5


def _cparams(*sem):
    return pltpu.CompilerParams(dimension_semantics=sem, vmem_limit_bytes=VMEM_LIMIT_BYTES)


def _dot(a, b):
    return jnp.dot(a, b, preferred_element_type=F32)


def _dot_nt(a, b):
    return lax.dot_general(a, b, (((1,), (1,)), ((), ())), preferred_element_type=F32)


def _dot_tn(a, b):
    return lax.dot_general(a, b, (((0,), (0,)), ((), ())), preferred_element_type=F32)


def _split_bf16(x):
    hi = x.astype(BF16)
    return hi, (x - hi.astype(F32)).astype(BF16)


def _silu(x):
    return x * jax.nn.sigmoid(x)


def _rms_mod(x, g, shift, scale):
    y = x * lax.rsqrt(jnp.mean(x * x, axis=-1, keepdims=True) + NORM_EPS) * g
    return y * (1 + scale) + shift


def _group_norm(o, axis):
    mu = jnp.mean(o, axis=axis, keepdims=True)
    d = o - mu
    return d * lax.rsqrt(jnp.mean(d * d, axis=axis, keepdims=True) + NORM_EPS)


def _stick_terms(z):
    e = jnp.exp(-jnp.abs(z))
    p = 1.0 + e
    r = 1.0 / p
    beta = jnp.where(z >= 0, r, e * r)
    lpos = jnp.maximum(z, 0.0) + jnp.log(p)
    return beta, lpos


def _suffix_matrix(n):
    j = jnp.arange(n)
    return jnp.concatenate([(j[:, None] >= j[None, :]), jnp.ones((n, n), bool)], axis=1).astype(BF16)


def _head_of_lane_mask(heads):
    shape = (heads, heads * HEAD_DIM)
    return lax.broadcasted_iota(I32, shape, 0) == lax.broadcasted_iota(I32, shape, 1) // HEAD_DIM


def _ada_kernel(c_ref, w_ref, b_ref, o_ref):
    a = _silu(c_ref[...]).astype(BF16)
    o_ref[...] = _dot(a, w_ref[...].astype(BF16)) + b_ref[...]


def _ada(c, w, b, tn=512):
    m, d = c.shape
    n = w.shape[1]
    return pl.pallas_call(
        _ada_kernel,
        out_shape=jax.ShapeDtypeStruct((m, n), F32),
        grid=(n // tn,),
        in_specs=[
            pl.BlockSpec((m, d), lambda j: (0, 0)),
            pl.BlockSpec((d, tn), lambda j: (0, j)),
            pl.BlockSpec((1, tn), lambda j: (0, j)),
        ],
        out_specs=pl.BlockSpec((m, tn), lambda j: (0, j)),
        compiler_params=_cparams("arbitrary"),
        name="ada",
    )(c, w, b.reshape(1, n))


def _swap_halves(x):
    lane = lax.broadcasted_iota(I32, x.shape, 1)
    fwd = pltpu.roll(x, V7X_LANES - HEAD_DIM // 2, 1)
    bwd = pltpu.roll(x, HEAD_DIM // 2, 1)
    return jnp.where(lane % HEAD_DIM < HEAD_DIM // 2, fwd, bwd)


def _proj_kernel(x_ref, sh_ref, sc_ref, g_ref, w_ref, wt_ref, cos_ref, sin_ref, *out_refs, width):
    hb = _rms_mod(x_ref[...], g_ref[...], sh_ref[...], sc_ref[...]).astype(BF16)
    cos, sin = cos_ref[...], sin_ref[...]
    for gi, o_ref in enumerate(out_refs[:N_ROW_GROUPS]):
        acc = _dot(hb, w_ref[:, gi * width:(gi + 1) * width])
        if gi < 2:
            for c in range(width // V7X_LANES):
                sl = slice(c * V7X_LANES, (c + 1) * V7X_LANES)
                t = acc[:, sl]
                t = t * cos + _swap_halves(t) * sin
                o_ref[:, sl] = t * (HEAD_DIM ** -0.5) if gi == 1 else t
        else:
            o_ref[...] = acc
    for gi, o_ref in enumerate(out_refs[N_ROW_GROUPS:]):
        o_ref[...] = _dot_nt(wt_ref[gi * width:(gi + 1) * width, :], hb)


def _proj(x, shift, scale, g, w_bf16, cos, sin, *, tm, rows_per_mod):
    t_rows, d = x.shape
    width = w_bf16.shape[1] // 7
    w_row = w_bf16[:, :N_ROW_GROUPS * width]
    w_t = w_bf16[:, N_ROW_GROUPS * width:].T
    if rows_per_mod is None:
        tps, rows = 1, t_rows
        mod_spec = pl.BlockSpec((tm, d), lambda i: (i, 0))
        rope_spec = pl.BlockSpec((1, V7X_LANES), lambda i: (0, 0))
    else:
        tps, rows = rows_per_mod // tm, rows_per_mod
        mod_spec = pl.BlockSpec((None, 1, d), lambda i: (i // tps, 0, 0))
        rope_spec = pl.BlockSpec((tm, V7X_LANES), lambda i: (i % tps, 0))
    row_sds = jax.ShapeDtypeStruct((t_rows, width), F32)
    t_sds = jax.ShapeDtypeStruct((t_rows // rows, width, rows), F32)
    return pl.pallas_call(
        functools.partial(_proj_kernel, width=width),
        out_shape=(row_sds,) * N_ROW_GROUPS + (t_sds,) * 2,
        grid=(t_rows // tm,),
        in_specs=[
            pl.BlockSpec((tm, d), lambda i: (i, 0)),
            mod_spec,
            mod_spec,
            pl.BlockSpec((1, d), lambda i: (0, 0)),
            pl.BlockSpec(w_row.shape, lambda i: (0, 0)),
            pl.BlockSpec(w_t.shape, lambda i: (0, 0)),
            rope_spec,
            rope_spec,
        ],
        out_specs=(pl.BlockSpec((tm, width), lambda i: (i, 0)),) * N_ROW_GROUPS
        + (pl.BlockSpec((None, width, tm), lambda i: (i // tps, 0, i % tps)),) * 2,
        compiler_params=_cparams("arbitrary"),
        name="proj",
    )(x, shift, scale, g, w_row, w_t, cos, sin)


def _rope_tables(pos):
    half = HEAD_DIM // 2
    freq = ROPE_BASE ** (-jnp.arange(half, dtype=F32) / half)
    ang = pos[:, None] * freq[None, :]
    cos, sin = jnp.cos(ang), jnp.sin(ang)
    reps = V7X_LANES // HEAD_DIM
    return jnp.tile(cos, (1, 2 * reps)), jnp.tile(jnp.concatenate([-sin, sin], axis=1), (1, reps))


def _log_gamma(heads):
    return jnp.log1p(-jnp.exp2(-5.0 - jnp.arange(heads, dtype=F32)))


def _ret_kernel(cd_ref, q_ref, k_ref, v_ref, g_ref, gn_ref, intra_ref, qd_ref, kd_ref, o_ref, s_ref, *, heads):
    @pl.when(pl.program_id(1) == 0)
    def _():
        s_ref[...] = jnp.zeros_like(s_ref)

    q, k, v = q_ref[...], k_ref[...], v_ref[...]
    qd = q * qd_ref[...]
    kd = k * kd_ref[...]
    outs = []
    for h in range(heads):
        sl = slice(h * HEAD_DIM, (h + 1) * HEAD_DIM)
        qh, kh, vh = q[:, sl].astype(BF16), k[:, sl].astype(BF16), v[:, sl].astype(BF16)
        s_old = s_ref[h]
        scores = _dot_nt(qh, kh) * intra_ref[h]
        o = _dot(scores.astype(BF16), vh) + _dot(qd[:, sl].astype(BF16), s_old.astype(BF16))
        s_ref[h] = s_old * cd_ref[h] + _dot_tn(kd[:, sl].astype(BF16), vh)
        outs.append(_group_norm(o, -1))
    o_ref[...] = jnp.concatenate(outs, axis=1) * gn_ref[...] * _silu(g_ref[...])


def _retention_prompt(q, k, v, g, ret_gn, batch, seq):
    width = q.shape[1]
    heads = width // HEAD_DIM
    c = RET_CHUNK
    assert seq % c == 0
    nc = seq // c
    lg = _log_gamma(heads)
    t = jnp.arange(c, dtype=F32)
    diff = t[:, None] - t[None, :]
    intra = jnp.exp(jnp.where(diff[None] >= 0, diff[None] * lg[:, None, None], -jnp.inf))
    q_dec = jnp.repeat(jnp.exp((t[:, None] + 1.0) * lg[None, :]), HEAD_DIM, axis=1)
    k_dec = jnp.repeat(jnp.exp((c - 1.0 - t[:, None]) * lg[None, :]), HEAD_DIM, axis=1)
    chunk_dec = jnp.exp(c * lg)
    row = pl.BlockSpec((c, width), lambda b, i: (b * nc + i, 0))
    const2 = pl.BlockSpec((c, width), lambda b, i: (0, 0))
    return pl.pallas_call(
        functools.partial(_ret_kernel, heads=heads),
        out_shape=(
            jax.ShapeDtypeStruct((batch * seq, width), F32),
            jax.ShapeDtypeStruct((batch, heads, HEAD_DIM, HEAD_DIM), F32),
        ),
        grid=(batch, nc),
        in_specs=[
            pl.BlockSpec(memory_space=pltpu.SMEM),
            row, row, row, row,
            pl.BlockSpec((1, width), lambda b, i: (0, 0)),
            pl.BlockSpec((heads, c, c), lambda b, i: (0, 0, 0)),
            const2, const2,
        ],
        out_specs=(
            row,
            pl.BlockSpec((None, heads, HEAD_DIM, HEAD_DIM), lambda b, i: (b, 0, 0, 0)),
        ),
        compiler_params=_cparams("arbitrary", "arbitrary"),
        name="retention_prompt",
    )(chunk_dec, q, k, v, g, ret_gn.reshape(1, width), intra, q_dec, k_dec)


def _ret_dec_kernel(gam_ref, q_ref, k_ref, v_ref, g_ref, gn_ref, s_ref, o_ref, so_ref):
    gamma = gam_ref[pl.program_id(0)]
    q, k, v = q_ref[...], k_ref[...], v_ref[...]
    qg = q * gamma
    o = jnp.sum(q * k, axis=0, keepdims=True) * v
    for d in range(s_ref.shape[0]):
        s_old = s_ref[d]
        so_ref[d] = s_old * gamma + k[d:d + 1, :] * v
        o = o + qg[d:d + 1, :] * s_old
    o_ref[...] = _group_norm(o, 0) * gn_ref[...] * _silu(g_ref[...])


def _retention_decode(q_t, k_t, v_t, g_t, state_t, ret_gn):
    width, b = q_t.shape
    hd = HEAD_DIM
    heads = width // hd
    gamma = jnp.exp(_log_gamma(heads))
    vec = pl.BlockSpec((hd, b), lambda h: (h, 0))
    st = pl.BlockSpec((None, None, hd, hd, b), lambda h: (0, h, 0, 0, 0))
    return pl.pallas_call(
        _ret_dec_kernel,
        out_shape=(jax.ShapeDtypeStruct((width, b), F32), jax.ShapeDtypeStruct(state_t.shape, F32)),
        grid=(heads,),
        in_specs=[pl.BlockSpec(memory_space=pltpu.SMEM), vec, vec, vec, vec, vec, st],
        out_specs=(vec, st),
        compiler_params=_cparams("arbitrary"),
        name="retention_decode",
    )(gamma, q_t, k_t, v_t, g_t, jnp.broadcast_to(ret_gn[:, None], (width, b)), state_t)


def _sbp_kernel(bias_ref, q_ref, k_ref, v_ref, tt_ref, o_ref, acc_scr, o_scr, *, heads):
    blk = SB_BLOCK
    qi = pl.program_id(1)
    acc_scr[...] = jnp.zeros_like(acc_scr)
    o_scr[...] = jnp.zeros_like(o_scr)
    q = q_ref[...] * (HEAD_DIM ** -0.5)
    q_heads = [q[:, h * HEAD_DIM:(h + 1) * HEAD_DIM].astype(BF16) for h in range(heads)]
    row = lax.broadcasted_iota(I32, (blk, blk), 0)
    col = lax.broadcasted_iota(I32, (blk, blk), 1)

    def key_block(j, masked):
        start = pl.multiple_of(j * blk, blk)
        kb = k_ref[:, pl.ds(start, blk)]
        vb = v_ref[:, pl.ds(start, blk)]
        valid = col < row
        betas, lposs = [], []
        for h in range(heads):
            sl = slice(h * HEAD_DIM, (h + 1) * HEAD_DIM)
            beta, lpos = _stick_terms(_dot(q_heads[h], kb[sl, :].astype(BF16)) + bias_ref[h])
            betas.append(beta)
            lposs.append(jnp.where(valid, lpos, 0.0) if masked else lpos)
        hi, lo = _split_bf16(jnp.concatenate(lposs, axis=0))
        tt = tt_ref[:, :blk]
        sums = _dot(hi, tt) + _dot(lo, tt)
        for h in range(heads):
            sl = slice(h * HEAD_DIM, (h + 1) * HEAD_DIM)
            rs = slice(h * blk, (h + 1) * blk)
            acc = acc_scr[h]
            w = betas[h] * jnp.exp(lposs[h] - sums[rs, :blk] - acc)
            if masked:
                w = jnp.where(valid, w, 0.0)
            o_scr[h] = o_scr[h] + _dot_nt(w.astype(BF16), vb[sl, :].astype(BF16))
            acc_scr[h] = acc + jnp.broadcast_to(sums[rs, :1], (blk, blk))

    key_block(qi, True)

    def body(jj, carry):
        key_block(qi - jj, False)
        return carry

    lax.fori_loop(1, qi + 1, body, 0)
    o_ref[...] = jnp.concatenate([o_scr[h] for h in range(heads)], axis=1)


def _sb_prompt(q, k_t, v_t, bias, seq):
    batch, width, _ = k_t.shape
    heads = width // HEAD_DIM
    blk = SB_BLOCK
    assert seq % blk == 0
    nq = seq // blk
    kv = pl.BlockSpec((None, width, seq), lambda b, i: (b, 0, 0))
    return pl.pallas_call(
        functools.partial(_sbp_kernel, heads=heads),
        out_shape=jax.ShapeDtypeStruct(q.shape, F32),
        grid=(batch, nq),
        in_specs=[
            pl.BlockSpec(memory_space=pltpu.SMEM),
            pl.BlockSpec((blk, width), lambda b, i: (b * nq + i, 0)),
            kv, kv,
            pl.BlockSpec((blk, 2 * blk), lambda b, i: (0, 0)),
        ],
        out_specs=pl.BlockSpec((blk, width), lambda b, i: (b * nq + i, 0)),
        scratch_shapes=[pltpu.VMEM((heads, blk, blk), F32), pltpu.VMEM((heads, blk, HEAD_DIM), F32)],
        compiler_params=_cparams("arbitrary", "arbitrary"),
        name="sb_prompt",
    )(bias, q, k_t, v_t, _suffix_matrix(blk))


def _sbd_kernel(pt_ref, q_ref, ks_ref, vs_ref, bias_ref, tt_ref, ck_ref, cv_ref, o_ref,
                k_buf, v_buf, sem, acc_scr, o_scr, *, heads, pages, page, n_pages, past_len):
    bi, s = pl.program_id(0), pl.program_id(1)
    n_steps = pl.num_programs(1)
    step = bi * n_steps + s
    slot = lax.rem(step, 2)
    width = heads * HEAD_DIM
    scale = HEAD_DIM ** -0.5
    q_pos = past_len

    def page_copies(row, sweep, buf_slot, i):
        phys = pt_ref[row, n_pages - 1 - (sweep * pages + i)]
        return (pltpu.make_async_copy(ck_ref.at[0, phys], k_buf.at[buf_slot, i], sem.at[buf_slot]),
                pltpu.make_async_copy(cv_ref.at[0, phys], v_buf.at[buf_slot, i], sem.at[buf_slot]))

    def start_fetch(row, sweep, buf_slot):
        for i in range(pages):
            for cp in page_copies(row, sweep, buf_slot, i):
                cp.start()

    @pl.when(step == 0)
    def _():
        start_fetch(0, 0, 0)

    @pl.when(step + 1 < pl.num_programs(0) * n_steps)
    def _():
        last = s + 1 == n_steps
        start_fetch(jnp.where(last, bi + 1, bi), jnp.where(last, 0, s + 1), 1 - slot)

    bd = _head_of_lane_mask(heads)
    q_bd = jnp.where(bd, q_ref[...], 0.0)
    bias = bias_ref[...]

    @pl.when(s == 0)
    def _():
        z = jnp.sum(q_bd * ks_ref[...], axis=1, keepdims=True) * scale + bias[:heads]
        valid = jnp.full(z.shape, past_len, I32) < q_pos
        beta, lpos = _stick_terms(z)
        lpos = jnp.where(valid, lpos, 0.0)
        w = jnp.where(valid, beta, 0.0)
        o_scr[...] = w * vs_ref[...]
        acc_scr[...] = jnp.broadcast_to(lpos, acc_scr.shape)

    for i in range(pages):
        for cp in page_copies(bi, s, slot, i):
            cp.wait()

    q_b = q_bd.astype(BF16)
    z = jnp.concatenate([_dot(q_b, k_buf[slot, i].reshape(width, page).astype(BF16)) for i in range(pages)],
                        axis=0) * scale + bias
    r = lax.broadcasted_iota(I32, z.shape, 0)
    lane = lax.broadcasted_iota(I32, z.shape, 1)
    logical_page = (n_pages - 1 - s * pages) - r // heads
    mask = (logical_page * page + lane) < q_pos
    beta, lpos = _stick_terms(z)
    lpos = jnp.where(mask, lpos, 0.0)
    hi, lo = _split_bf16(lpos)
    sums = _dot(hi, tt_ref[...]) + _dot(lo, tt_ref[...])
    carry = acc_scr[...]
    carries = []
    for i in range(pages):
        carries.append(carry)
        carry = carry + sums[i * heads:(i + 1) * heads, page:]
    acc_scr[...] = carry
    rev = sums[:, :page] + jnp.concatenate(carries, axis=0)
    w = jnp.where(mask, beta * jnp.exp(lpos - rev), 0.0)
    o = o_scr[...]
    for i in range(pages):
        o = o + _dot_nt(w[i * heads:(i + 1) * heads].astype(BF16),
                        v_buf[slot, i].reshape(width, page).astype(BF16))
    o_scr[...] = o

    @pl.when(s == n_steps - 1)
    def _():
        o_ref[...] = jnp.sum(jnp.where(bd, o_scr[...], 0.0), axis=0, keepdims=True)


def _sb_decode(q, k_self, v_self, bias, cache_k_t, cache_v_t, page_table):
    b, _, width = q.shape
    _, _, heads, hd, page = cache_k_t.shape
    n_pages = page_table.shape[1]
    pages = min(DEC_PAGES, n_pages)
    assert n_pages % pages == 0 and page == V7X_LANES and heads * hd == width
    vec = pl.BlockSpec((None, 1, width), lambda bi, s, pt: (bi, 0, 0))
    page_buf = pltpu.VMEM((2, pages, heads, hd, page), F32)
    grid_spec = pltpu.PrefetchScalarGridSpec(
        num_scalar_prefetch=1,
        grid=(b, n_pages // pages),
        in_specs=[
            vec, vec, vec,
            pl.BlockSpec((pages * heads, 1), lambda bi, s, pt: (0, 0)),
            pl.BlockSpec((page, 2 * page), lambda bi, s, pt: (0, 0)),
            pl.BlockSpec(memory_space=pl.ANY),
            pl.BlockSpec(memory_space=pl.ANY),
        ],
        out_specs=vec,
        scratch_shapes=[page_buf, page_buf, pltpu.SemaphoreType.DMA((2,)),
                        pltpu.VMEM((heads, page), F32), pltpu.VMEM((heads, width), F32)],
    )
    return pl.pallas_call(
        functools.partial(_sbd_kernel, heads=heads, pages=pages, page=page, n_pages=n_pages,
                          past_len=n_pages * page),
        out_shape=jax.ShapeDtypeStruct(q.shape, F32),
        grid_spec=grid_spec,
        compiler_params=_cparams("arbitrary", "arbitrary"),
        name="sb_decode",
    )(page_table, q, k_self, v_self, jnp.tile(bias, pages).reshape(pages * heads, 1), _suffix_matrix(page),
      cache_k_t, cache_v_t)


def _out_kernel(or_ref, os_ref, x_ref, gt_ref, sh_ref, sc_ref, g_ref, w_ref, wrh_ref, wrl_ref, br_ref,
                x1_ref, h2_ref, lg_ref):
    half = or_ref.shape[1]
    mixed = _dot(or_ref[...].astype(BF16), w_ref[:half, :]) + _dot(os_ref[...].astype(BF16), w_ref[half:, :])
    x1 = x_ref[...] + gt_ref[...] * mixed
    x1_ref[...] = x1
    h2 = _rms_mod(x1, g_ref[...], sh_ref[...], sc_ref[...])
    h2_ref[...] = h2
    hh, hl = _split_bf16(h2)
    lg_ref[...] = (_dot_nt(wrh_ref[...], hh) + _dot_nt(wrh_ref[...], hl) + _dot_nt(wrl_ref[...], hh)
                   + br_ref[...])


def _out_proj(o_r, o_s, x, gate, shift, scale, g, w_bf16, w_router, b_router, *, tm, rows_per_mod):
    t_rows, d = x.shape
    half = o_r.shape[1]
    n_exp = w_router.shape[1]
    if rows_per_mod is None:
        mod_spec = pl.BlockSpec((tm, d), lambda i: (i, 0))
    else:
        tps = rows_per_mod // tm
        mod_spec = pl.BlockSpec((None, 1, d), lambda i: (i // tps, 0, 0))
    wr_hi, wr_lo = _split_bf16(w_router.T)
    const = lambda shape: pl.BlockSpec(shape, lambda i: (0, 0))
    return pl.pallas_call(
        _out_kernel,
        out_shape=(
            jax.ShapeDtypeStruct((t_rows, d), F32),
            jax.ShapeDtypeStruct((t_rows, d), F32),
            jax.ShapeDtypeStruct((n_exp, t_rows), F32),
        ),
        grid=(t_rows // tm,),
        in_specs=[
            pl.BlockSpec((tm, half), lambda i: (i, 0)),
            pl.BlockSpec((tm, half), lambda i: (i, 0)),
            pl.BlockSpec((tm, d), lambda i: (i, 0)),
            mod_spec, mod_spec, mod_spec,
            const((1, d)), const(w_bf16.shape), const((n_exp, d)), const((n_exp, d)), const((n_exp, 1)),
        ],
        out_specs=(
            pl.BlockSpec((tm, d), lambda i: (i, 0)),
            pl.BlockSpec((tm, d), lambda i: (i, 0)),
            pl.BlockSpec((n_exp, tm), lambda i: (0, i)),
        ),
        compiler_params=_cparams("arbitrary"),
        name="out_proj",
    )(o_r, o_s, x, gate, shift, scale, g, w_bf16, wr_hi, wr_lo, b_router.reshape(n_exp, 1))


def _route_kernel(lg_ref, tri_e_ref, upper_ref, dest_ref, gate_ref, be_ref, zf_ref, na_ref,
                  cnt_scr, base_scr, start_scr, *, tm):
    ph, i = pl.program_id(0), pl.program_id(1)
    logits = lg_ref[...]
    n_exp, tn = logits.shape
    e_iota = lax.broadcasted_iota(I32, (n_exp, tn), 0)
    work = logits
    sels, vals = [], []
    for _ in range(TOP_K):
        m = jnp.max(work, axis=0, keepdims=True)
        idx = jnp.min(jnp.where(work == m, e_iota, n_exp), axis=0, keepdims=True)
        sel = e_iota == idx
        sels.append(sel)
        vals.append(m)
        work = jnp.where(sel, -jnp.inf, work)
    chosen = sum(jnp.where(sel, 1.0, 0.0) for sel in sels)
    tile_count = jnp.sum(chosen, axis=1, keepdims=True)

    @pl.when(ph == 0)
    def _():
        @pl.when(i == 0)
        def _():
            cnt_scr[...] = jnp.zeros_like(cnt_scr)

        cnt_scr[...] = cnt_scr[...] + tile_count

    @pl.when((ph == 1) & (i == 0))
    def _():
        nblk = jnp.floor((cnt_scr[...] + (tm - 1)) * (1.0 / tm))
        first = _dot(tri_e_ref[...], nblk.astype(BF16))
        last = first + nblk
        start_scr[...] = first * tm
        base_scr[...] = jnp.zeros_like(base_scr)
        n_act = jnp.sum(nblk, axis=0, keepdims=True)
        na_ref[...] = n_act.astype(I32)
        for cb in range(be_ref.shape[1] // V7X_LANES):
            sl = slice(cb * V7X_LANES, (cb + 1) * V7X_LANES)
            bidx = (lax.broadcasted_iota(I32, (1, V7X_LANES), 1) + cb * V7X_LANES).astype(F32)
            owner = jnp.sum(jnp.where(last <= bidx, 1.0, 0.0), axis=0, keepdims=True)
            be_ref[:, sl] = jnp.minimum(owner, n_exp - 1.0).astype(I32)
            is_last = jnp.sum(jnp.where((last == bidx + 1.0) & (nblk > 0), 1.0, 0.0), axis=0, keepdims=True)
            zf_ref[:, sl] = jnp.where((is_last > 0) | (bidx >= n_act), 1, 0).astype(I32)

    @pl.when(ph == 1)
    def _():
        before = _dot(chosen.astype(BF16), upper_ref[...])
        row_of = start_scr[:, :1] + base_scr[:, :1] + before
        exps = [jnp.exp(v - vals[0]) for v in vals]
        denom = sum(exps)
        for kk in range(TOP_K):
            dest_ref[kk:kk + 1, :] = jnp.sum(jnp.where(sels[kk], row_of, 0.0), axis=0, keepdims=True).astype(I32)
            gate_ref[kk:kk + 1, :] = exps[kk] / denom
        base_scr[...] = base_scr[...] + tile_count


def _route(logits_t, tm, n_blocks):
    n_exp, t = logits_t.shape
    tn = max(c for c in (512, 384, 256, 128) if t % c == 0)
    nt = t // tn
    nb_pad = -(-n_blocks // V7X_LANES) * V7X_LANES
    assert t // tm + 1 < 256
    e = jnp.arange(n_exp)
    tri_e = (e[None, :] < e[:, None]).astype(BF16)
    tt = jnp.arange(tn)
    upper = (tt[:, None] < tt[None, :]).astype(BF16)
    tok = lambda ph, i: (0, i * ph)
    const = lambda ph, i: (0, 0)
    return pl.pallas_call(
        functools.partial(_route_kernel, tm=tm),
        out_shape=(
            jax.ShapeDtypeStruct((TOP_K, t), I32),
            jax.ShapeDtypeStruct((TOP_K, t), F32),
            jax.ShapeDtypeStruct((1, nb_pad), I32),
            jax.ShapeDtypeStruct((1, nb_pad), I32),
            jax.ShapeDtypeStruct((1, V7X_LANES), I32),
        ),
        grid=(2, nt),
        in_specs=[
            pl.BlockSpec((n_exp, tn), lambda ph, i: (0, i)),
            pl.BlockSpec((n_exp, n_exp), const),
            pl.BlockSpec((tn, tn), const),
        ],
        out_specs=(
            pl.BlockSpec((TOP_K, tn), tok),
            pl.BlockSpec((TOP_K, tn), tok),
            pl.BlockSpec((1, nb_pad), const),
            pl.BlockSpec((1, nb_pad), const),
            pl.BlockSpec((1, V7X_LANES), const),
        ),
        scratch_shapes=[pltpu.VMEM((n_exp, V7X_LANES), F32)] * 3,
        compiler_params=_cparams("arbitrary", "arbitrary"),
        name="route",
    )(logits_t, tri_e, upper)


def _for_each_token(groups, fn):
    for grp in range(groups):
        def body(lane, carry, grp=grp):
            fn(grp, lane, grp * V7X_LANES + lane)
            return carry

        lax.fori_loop(0, V7X_LANES, body, 0, unroll=2)


def _dispatch_kernel(zf_ref, dest_ref, h_ref, xb_ref, zero_scr, sem, zsem, *, tm, n_blocks):
    td = h_ref.shape[0]

    def zero_copy(b):
        return pltpu.make_async_copy(zero_scr, xb_ref.at[pl.ds(b * tm, tm)], zsem)

    @pl.when(pl.program_id(0) == 0)
    def _():
        zero_scr[...] = jnp.zeros_like(zero_scr)

        def start(b, c):
            @pl.when(zf_ref[b] > 0)
            def _():
                zero_copy(b).start()
            return c

        def wait(b, c):
            @pl.when(zf_ref[b] > 0)
            def _():
                zero_copy(b).wait()
            return c

        lax.fori_loop(0, n_blocks, start, 0)
        lax.fori_loop(0, n_blocks, wait, 0)

    def row_copy(t, dst_row):
        return pltpu.make_async_copy(h_ref.at[pl.ds(t, 1)], xb_ref.at[pl.ds(dst_row, 1)], sem)

    def start_tok(grp, lane, t):
        for kk in range(TOP_K):
            row_copy(t, dest_ref[grp, kk, lane]).start(priority=kk % 2)

    def wait_tok(grp, lane, t):
        for kk in range(TOP_K):
            row_copy(0, 0).wait()

    _for_each_token(td // V7X_LANES, start_tok)
    _for_each_token(td // V7X_LANES, wait_tok)


def _dispatch(h2, dest3, zero_flags, tm, n_blocks):
    t, d = h2.shape
    groups = dest3.shape[0]
    m = max(c for c in (4, 3, 2, 1) if groups % c == 0)
    td = m * V7X_LANES
    grid_spec = pltpu.PrefetchScalarGridSpec(
        num_scalar_prefetch=1,
        grid=(t // td,),
        in_specs=[
            pl.BlockSpec((m, TOP_K, V7X_LANES), lambda i, zf: (i, 0, 0), memory_space=pltpu.SMEM),
            pl.BlockSpec((td, d), lambda i, zf: (i, 0)),
        ],
        out_specs=pl.BlockSpec(memory_space=pl.ANY),
        scratch_shapes=[pltpu.VMEM((tm, d), F32), pltpu.SemaphoreType.DMA, pltpu.SemaphoreType.DMA],
    )
    return pl.pallas_call(
        functools.partial(_dispatch_kernel, tm=tm, n_blocks=n_blocks),
        out_shape=jax.ShapeDtypeStruct((n_blocks * tm, d), F32),
        grid_spec=grid_spec,
        compiler_params=_cparams("arbitrary"),
        name="dispatch",
    )(zero_flags, dest3, h2)


def _expert_kernel(be_ref, na_ref, x_ref, wgu_ref, bgu_ref, wdn_ref, bdn_ref, o_ref):
    i = pl.program_id(0)
    d_exp = wdn_ref.shape[0]

    @pl.when(i < na_ref[0])
    def _():
        h = _dot(x_ref[...].astype(BF16), wgu_ref[...].astype(BF16)) + bgu_ref[...]
        g = jnp.minimum(h[:, :d_exp], SWIGLU_LIMIT)
        u = jnp.clip(h[:, d_exp:], -SWIGLU_LIMIT, SWIGLU_LIMIT)
        a = g * jax.nn.sigmoid(SWIGLU_ALPHA * g) * (u + 1)
        o_ref[...] = _dot(a.astype(BF16), wdn_ref[...].astype(BF16)) + bdn_ref[...]

    @pl.when(i >= na_ref[0])
    def _():
        o_ref[...] = jnp.zeros_like(o_ref)


def _experts(xb, block_e, n_act, w_gu, b_gu, w_dn, b_dn, tm):
    n_rows, d = xb.shape
    _, n_exp, _, two_de = w_gu.shape
    d_exp = two_de // 2
    nb = n_rows // tm
    act = lambda i, na: jnp.minimum(i, na[0] - 1)
    grid_spec = pltpu.PrefetchScalarGridSpec(
        num_scalar_prefetch=2,
        grid=(nb,),
        in_specs=[
            pl.BlockSpec((tm, d), lambda i, be, na: (act(i, na), 0)),
            pl.BlockSpec((None, None, d, two_de), lambda i, be, na: (0, be[act(i, na)], 0, 0)),
            pl.BlockSpec((None, 1, two_de), lambda i, be, na: (be[act(i, na)], 0, 0)),
            pl.BlockSpec((None, None, d_exp, d), lambda i, be, na: (0, be[act(i, na)], 0, 0)),
            pl.BlockSpec((None, 1, d), lambda i, be, na: (be[act(i, na)], 0, 0)),
        ],
        out_specs=pl.BlockSpec((tm, d), lambda i, be, na: (i, 0)),
    )
    return pl.pallas_call(
        _expert_kernel,
        out_shape=jax.ShapeDtypeStruct((n_rows, d), F32),
        grid_spec=grid_spec,
        compiler_params=_cparams("arbitrary"),
        name="experts",
    )(block_e, n_act, xb, w_gu, b_gu.reshape(n_exp, 1, two_de), w_dn, b_dn.reshape(n_exp, 1, d))


def _final_kernel(dest_ref, x1_ref, gates_ref, gt_ref, sh_ref, sc_ref, g_ref, yb_ref, y_ref, rows_scr, sem):
    tf = x1_ref.shape[0]

    def row_copy(src_row, kk, t):
        return pltpu.make_async_copy(yb_ref.at[pl.ds(src_row, 1)], rows_scr.at[kk, pl.ds(t, 1)], sem)

    def start_tok(grp, lane, t):
        for kk in range(TOP_K):
            row_copy(dest_ref[grp, kk, lane], kk, t).start(priority=kk % 2)

    def wait_tok(grp, lane, t):
        for kk in range(TOP_K):
            row_copy(0, 0, 0).wait()

    _for_each_token(tf // V7X_LANES, start_tok)
    _for_each_token(tf // V7X_LANES, wait_tok)
    gates = gates_ref[...]
    moe = sum(gates[:, kk:kk + 1] * rows_scr[kk] for kk in range(TOP_K))
    x2 = x1_ref[...] + gt_ref[...] * moe
    y_ref[...] = _rms_mod(x2, g_ref[...], sh_ref[...], sc_ref[...])


def _final(x1, dest3, gates_t, gate2, shift, scale, g, yb, *, tok0, tf, rows_per_mod):
    t_rows, d = x1.shape
    m = tf // V7X_LANES
    assert tok0 % tf == 0
    first = tok0 // tf
    if rows_per_mod is None:
        mod_spec = pl.BlockSpec((tf, d), lambda i: (i, 0))
    else:
        tps = rows_per_mod // tf
        mod_spec = pl.BlockSpec((None, 1, d), lambda i: (i // tps, 0, 0))
    return pl.pallas_call(
        _final_kernel,
        out_shape=jax.ShapeDtypeStruct((t_rows, d), F32),
        grid=(t_rows // tf,),
        in_specs=[
            pl.BlockSpec((m, TOP_K, V7X_LANES), lambda i: (first + i, 0, 0), memory_space=pltpu.SMEM),
            pl.BlockSpec((tf, d), lambda i: (i, 0)),
            pl.BlockSpec((tf, TOP_K), lambda i: (first + i, 0)),
            mod_spec, mod_spec, mod_spec,
            pl.BlockSpec((1, d), lambda i: (0, 0)),
            pl.BlockSpec(memory_space=pl.ANY),
        ],
        out_specs=pl.BlockSpec((tf, d), lambda i: (i, 0)),
        scratch_shapes=[pltpu.VMEM((TOP_K, tf, d), F32), pltpu.SemaphoreType.DMA],
        compiler_params=_cparams("arbitrary"),
        name="final",
    )(dest3, x1, gates_t, gate2, shift, scale, g, yb)


def kernel(x_prompt, x_sample, cache_k, cache_v, state_ret, page_table, c_prompt, c_sample, w_ada, b_ada, g_mix, g_ffn, w_in, sb_bias, ret_gn, w_out, w_router, b_router, w_gate_up, b_gate_up, w_down, b_down, w_ada_final, b_ada_final, g_final):
    batch, seq, d = x_prompt.shape
    dec_b, dec_seq, _ = x_sample.shape
    depth = w_ada.shape[0]
    assert depth == 1 and dec_seq == 1
    page = cache_k.shape[2]
    ret_w = ret_gn.shape[1]
    sb_w = w_out.shape[1] - ret_w
    assert ret_w == sb_w
    sb_h = sb_w // HEAD_DIM
    n_exp = w_router.shape[2]
    tp, ts = batch * seq, dec_b
    tm_p = min(ROW_TILE, seq)
    past_len = page_table.shape[1] * page

    c_all = jnp.concatenate([c_prompt, c_sample], axis=0)
    mod = _ada(c_all, w_ada.reshape(w_ada.shape[1:]), b_ada[0])
    fin = _ada(c_all, w_ada_final, b_ada_final)
    mods_p = [mod[:batch, i * d:(i + 1) * d].reshape(batch, 1, d) for i in range(6)]
    mods_s = [mod[batch:, i * d:(i + 1) * d] for i in range(6)]
    fin_p = [fin[:batch, i * d:(i + 1) * d].reshape(batch, 1, d) for i in range(2)]
    fin_s = [fin[batch:, i * d:(i + 1) * d] for i in range(2)]

    w_in_b = w_in[0].astype(BF16)
    w_out_b = w_out[0].astype(BF16)
    g_mix2, g_ffn2, g_fin2 = g_mix.reshape(1, d), g_ffn.reshape(1, d), g_final.reshape(1, d)

    cos_p, sin_p = _rope_tables(jnp.arange(seq, dtype=F32))
    xp = x_prompt.reshape(tp, d)
    q_r, k_r, v_r, g_r, q_s, k_st, v_st = _proj(xp, mods_p[0], mods_p[1], g_mix2, w_in_b, cos_p, sin_p,
                                                tm=tm_p, rows_per_mod=seq)
    o_r, ret_p = _retention_prompt(q_r, k_r, v_r, g_r, ret_gn[0], batch, seq)
    o_s = _sb_prompt(q_s, k_st, v_st, sb_bias[0], seq)
    x1_p, h2_p, lg_p = _out_proj(o_r, o_s, xp, mods_p[2], mods_p[3], mods_p[4], g_ffn2, w_out_b,
                                 w_router[0], b_router[0], tm=tm_p, rows_per_mod=seq)

    cos_s, sin_s = _rope_tables(past_len + jnp.arange(1, dtype=F32))
    xs = x_sample.reshape(ts, d)
    sq_r, sk_r, sv_r, sg_r, sq_s, sk_st, sv_st = _proj(xs, mods_s[0], mods_s[1], g_mix2, w_in_b, cos_s, sin_s,
                                                       tm=ts, rows_per_mod=None)
    state_t = state_ret.transpose(0, 2, 3, 4, 1)
    so_rt, ret_st = _retention_decode(sq_r.T, sk_r.T, sv_r.T, sg_r.T, state_t, ret_gn[0])
    rows3 = lambda a: a.reshape(ts, 1, sb_w)
    so_s = _sb_decode(rows3(sq_s), rows3(sk_st[0].T), rows3(sv_st[0].T), sb_bias[0],
                      cache_k.transpose(0, 1, 3, 4, 2), cache_v.transpose(0, 1, 3, 4, 2), page_table)
    x1_s, h2_s, lg_s = _out_proj(so_rt.T, so_s.reshape(ts, sb_w), xs, mods_s[2], mods_s[3], mods_s[4],
                                 g_ffn2, w_out_b, w_router[0], b_router[0], tm=ts, rows_per_mod=None)

    tm = MOE_ROWS
    t_all = tp + ts
    assert t_all % V7X_LANES == 0 and tp % V7X_LANES == 0
    n_blocks = -(-(t_all * TOP_K) // tm) + n_exp
    h2_all = jnp.concatenate([h2_p, h2_s], axis=0)
    lg_all = jnp.concatenate([lg_p, lg_s], axis=1)
    dest, gates, block_e, zero_flags, n_act = _route(lg_all, tm, n_blocks)
    dest3 = dest.reshape(TOP_K, t_all // V7X_LANES, V7X_LANES).transpose(1, 0, 2)
    gates_t = gates.T
    xb = _dispatch(h2_all, dest3, zero_flags[0, :n_blocks], tm, n_blocks)
    yb = _experts(xb, block_e[0, :n_blocks], n_act[0, :1], w_gate_up, b_gate_up[0], w_down, b_down[0], tm)

    tf_p = min(256, seq)
    y_p = _final(x1_p, dest3, gates_t, mods_p[5], fin_p[0], fin_p[1], g_fin2, yb, tok0=0, tf=tf_p,
                 rows_per_mod=seq)
    y_s = _final(x1_s, dest3, gates_t, mods_s[5], fin_s[0], fin_s[1], g_fin2, yb, tok0=tp, tf=ts,
                 rows_per_mod=None)

    kv_rows = lambda a, b, l: a.reshape(1, b, sb_h, HEAD_DIM, l).transpose(0, 1, 4, 2, 3)
    return (
        y_p.reshape(batch, seq, d),
        y_s.reshape(dec_b, 1, d),
        kv_rows(k_st, batch, seq),
        kv_rows(v_st, batch, seq),
        ret_p[None],
        kv_rows(sk_st, 1, dec_b).transpose(0, 2, 1, 3, 4),
        kv_rows(sv_st, 1, dec_b).transpose(0, 2, 1, 3, 4),
        ret_st.transpose(0, 4, 1, 2, 3),
    )
```

```python
import functools

import jax
import jax.numpy as jnp
from jax import lax
from jax.experimental import pallas as pl
from jax.experimental.pallas import tpu as pltpu

F32, BF16, I32 = jnp.float32, jnp.bfloat16, jnp.int32

HEAD_DIM = 64
RET_CHUNK = 128
SB_BLOCK = 128
ROPE_BASE = 10000.0
TOP_K = 4
SWIGLU_LIMIT = 7.0
SWIGLU_ALPHA = 1.702
NORM_EPS = 1e-6

V7X_LANES = 128
V7X_VMEM_BYTES = 64 * 2**20
VMEM_LIMIT_BYTES = V7X_VMEM_BYTES - 8 * 2**20

ROW_TILE = 512
MOE_ROWS = 256
DEC_PAGES = 16
RET_STEP_CHUNKS = 2
N_ROW_GROUPS = 5


def _cparams(*sem):
    return pltpu.CompilerParams(dimension_semantics=sem, vmem_limit_bytes=VMEM_LIMIT_BYTES)


def _dot(a, b):
    return jnp.dot(a, b, preferred_element_type=F32)


def _dot_nt(a, b):
    return lax.dot_general(a, b, (((1,), (1,)), ((), ())), preferred_element_type=F32)


def _dot_tn(a, b):
    return lax.dot_general(a, b, (((0,), (0,)), ((), ())), preferred_element_type=F32)


def _split_bf16(x):
    hi = x.astype(BF16)
    return hi, (x - hi.astype(F32)).astype(BF16)


def _silu(x):
    return x * jax.nn.sigmoid(x)


def _rms_mod(x, g, shift, scale):
    y = x * lax.rsqrt(jnp.mean(x * x, axis=-1, keepdims=True) + NORM_EPS) * g
    return y * (1 + scale) + shift


def _group_norm(o, axis):
    mu = jnp.mean(o, axis=axis, keepdims=True)
    d = o - mu
    return d * lax.rsqrt(jnp.mean(d * d, axis=axis, keepdims=True) + NORM_EPS)


def _stick_terms(z):
    e = jnp.exp(-jnp.abs(z))
    p = 1.0 + e
    r = 1.0 / p
    beta = jnp.where(z >= 0, r, e * r)
    lpos = jnp.maximum(z, 0.0) + jnp.log(p)
    return beta, lpos


def _suffix_matrix(n):
    j = jnp.arange(n)
    return jnp.concatenate([(j[:, None] >= j[None, :]), jnp.ones((n, n), bool)], axis=1).astype(BF16)


def _head_of_lane_mask(heads):
    shape = (heads, heads * HEAD_DIM)
    return lax.broadcasted_iota(I32, shape, 0) == lax.broadcasted_iota(I32, shape, 1) // HEAD_DIM


def _ada_kernel(c_ref, w_ref, b_ref, o_ref):
    a = _silu(c_ref[...]).astype(BF16)
    o_ref[...] = _dot(a, w_ref[...].astype(BF16)) + b_ref[...]


def _ada(c, w, b, tn=512):
    m, d = c.shape
    n = w.shape[1]
    return pl.pallas_call(
        _ada_kernel,
        out_shape=jax.ShapeDtypeStruct((m, n), F32),
        grid=(n // tn,),
        in_specs=[
            pl.BlockSpec((m, d), lambda j: (0, 0)),
            pl.BlockSpec((d, tn), lambda j: (0, j)),
            pl.BlockSpec((1, tn), lambda j: (0, j)),
        ],
        out_specs=pl.BlockSpec((m, tn), lambda j: (0, j)),
        compiler_params=_cparams("arbitrary"),
        name="ada",
    )(c, w, b.reshape(1, n))


def _swap_halves(x):
    lane = lax.broadcasted_iota(I32, x.shape, 1)
    fwd = pltpu.roll(x, V7X_LANES - HEAD_DIM // 2, 1)
    bwd = pltpu.roll(x, HEAD_DIM // 2, 1)
    return jnp.where(lane % HEAD_DIM < HEAD_DIM // 2, fwd, bwd)


def _proj_kernel(x_ref, sh_ref, sc_ref, g_ref, w_ref, wt_ref, cos_ref, sin_ref, *out_refs, width):
    hb = _rms_mod(x_ref[...], g_ref[...], sh_ref[...], sc_ref[...]).astype(BF16)
    cos, sin = cos_ref[...], sin_ref[...]
    for gi, o_ref in enumerate(out_refs[:N_ROW_GROUPS]):
        acc = _dot(hb, w_ref[:, gi * width:(gi + 1) * width])
        if gi < 2:
            for c in range(width // V7X_LANES):
                sl = slice(c * V7X_LANES, (c + 1) * V7X_LANES)
                t = acc[:, sl]
                t = t * cos + _swap_halves(t) * sin
                o_ref[:, sl] = t * (HEAD_DIM ** -0.5) if gi == 1 else t
        else:
            o_ref[...] = acc
    for gi, o_ref in enumerate(out_refs[N_ROW_GROUPS:]):
        o_ref[...] = _dot_nt(wt_ref[gi * width:(gi + 1) * width, :], hb)


def _proj(x, shift, scale, g, w_bf16, cos, sin, *, tm, rows_per_mod):
    t_rows, d = x.shape
    width = w_bf16.shape[1] // 7
    w_row = w_bf16[:, :N_ROW_GROUPS * width]
    w_t = w_bf16[:, N_ROW_GROUPS * width:].T
    if rows_per_mod is None:
        tps, rows = 1, t_rows
        mod_spec = pl.BlockSpec((tm, d), lambda i: (i, 0))
        rope_spec = pl.BlockSpec((1, V7X_LANES), lambda i: (0, 0))
    else:
        tps, rows = rows_per_mod // tm, rows_per_mod
        mod_spec = pl.BlockSpec((None, 1, d), lambda i: (i // tps, 0, 0))
        rope_spec = pl.BlockSpec((tm, V7X_LANES), lambda i: (i % tps, 0))
    row_sds = jax.ShapeDtypeStruct((t_rows, width), F32)
    t_sds = jax.ShapeDtypeStruct((t_rows // rows, width, rows), F32)
    return pl.pallas_call(
        functools.partial(_proj_kernel, width=width),
        out_shape=(row_sds,) * N_ROW_GROUPS + (t_sds,) * 2,
        grid=(t_rows // tm,),
        in_specs=[
            pl.BlockSpec((tm, d), lambda i: (i, 0)),
            mod_spec,
            mod_spec,
            pl.BlockSpec((1, d), lambda i: (0, 0)),
            pl.BlockSpec(w_row.shape, lambda i: (0, 0)),
            pl.BlockSpec(w_t.shape, lambda i: (0, 0)),
            rope_spec,
            rope_spec,
        ],
        out_specs=(pl.BlockSpec((tm, width), lambda i: (i, 0)),) * N_ROW_GROUPS
        + (pl.BlockSpec((None, width, tm), lambda i: (i // tps, 0, i % tps)),) * 2,
        compiler_params=_cparams("arbitrary"),
        name="proj",
    )(x, shift, scale, g, w_row, w_t, cos, sin)


def _rope_tables(pos):
    half = HEAD_DIM // 2
    freq = ROPE_BASE ** (-jnp.arange(half, dtype=F32) / half)
    ang = pos[:, None] * freq[None, :]
    cos, sin = jnp.cos(ang), jnp.sin(ang)
    reps = V7X_LANES // HEAD_DIM
    return jnp.tile(cos, (1, 2 * reps)), jnp.tile(jnp.concatenate([-sin, sin], axis=1), (1, reps))


def _log_gamma(heads):
    return jnp.log1p(-jnp.exp2(-5.0 - jnp.arange(heads, dtype=F32)))


def _ret_kernel(cd_ref, q_ref, k_ref, v_ref, g_ref, gn_ref, intra_ref, qd_ref, kd_ref, o_ref, s_ref, *, heads):
    @pl.when(pl.program_id(1) == 0)
    def _():
        s_ref[...] = jnp.zeros_like(s_ref)

    c = RET_CHUNK
    for ci in range(q_ref.shape[0] // c):
        rows = slice(ci * c, (ci + 1) * c)
        q, k, v = q_ref[rows, :], k_ref[rows, :], v_ref[rows, :]
        qd = q * qd_ref[...]
        kd = k * kd_ref[...]
        outs = []
        for h in range(heads):
            sl = slice(h * HEAD_DIM, (h + 1) * HEAD_DIM)
            qh, kh, vh = q[:, sl].astype(BF16), k[:, sl].astype(BF16), v[:, sl].astype(BF16)
            s_old = s_ref[h]
            scores = _dot_nt(qh, kh) * intra_ref[h]
            o = _dot(scores.astype(BF16), vh) + _dot(qd[:, sl].astype(BF16), s_old.astype(BF16))
            s_ref[h] = s_old * cd_ref[h] + _dot_tn(kd[:, sl].astype(BF16), vh)
            outs.append(_group_norm(o, -1))
        o_ref[rows, :] = jnp.concatenate(outs, axis=1) * gn_ref[...] * _silu(g_ref[rows, :])


def _retention_prompt(q, k, v, g, ret_gn, batch, seq):
    width = q.shape[1]
    heads = width // HEAD_DIM
    c = RET_CHUNK
    per_step = RET_STEP_CHUNKS if seq % (RET_STEP_CHUNKS * c) == 0 else 1
    assert seq % c == 0
    nc = seq // (per_step * c)
    lg = _log_gamma(heads)
    t = jnp.arange(c, dtype=F32)
    diff = t[:, None] - t[None, :]
    intra = jnp.exp(jnp.where(diff[None] >= 0, diff[None] * lg[:, None, None], -jnp.inf))
    q_dec = jnp.repeat(jnp.exp((t[:, None] + 1.0) * lg[None, :]), HEAD_DIM, axis=1)
    k_dec = jnp.repeat(jnp.exp((c - 1.0 - t[:, None]) * lg[None, :]), HEAD_DIM, axis=1)
    chunk_dec = jnp.exp(c * lg)
    row = pl.BlockSpec((per_step * c, width), lambda b, i: (b * nc + i, 0))
    const2 = pl.BlockSpec((c, width), lambda b, i: (0, 0))
    return pl.pallas_call(
        functools.partial(_ret_kernel, heads=heads),
        out_shape=(
            jax.ShapeDtypeStruct((batch * seq, width), F32),
            jax.ShapeDtypeStruct((batch, heads, HEAD_DIM, HEAD_DIM), F32),
        ),
        grid=(batch, nc),
        in_specs=[
            pl.BlockSpec(memory_space=pltpu.SMEM),
            row, row, row, row,
            pl.BlockSpec((1, width), lambda b, i: (0, 0)),
            pl.BlockSpec((heads, c, c), lambda b, i: (0, 0, 0)),
            const2, const2,
        ],
        out_specs=(
            row,
            pl.BlockSpec((None, heads, HEAD_DIM, HEAD_DIM), lambda b, i: (b, 0, 0, 0)),
        ),
        compiler_params=_cparams("arbitrary", "arbitrary"),
        name="retention_prompt",
    )(chunk_dec, q, k, v, g, ret_gn.reshape(1, width), intra, q_dec, k_dec)


def _ret_dec_kernel(gam_ref, q_ref, k_ref, v_ref, g_ref, gn_ref, s_ref, o_ref, so_ref):
    gamma = gam_ref[pl.program_id(0)]
    q, k, v = q_ref[...], k_ref[...], v_ref[...]
    qg = q * gamma
    o = jnp.sum(q * k, axis=0, keepdims=True) * v
    for d in range(s_ref.shape[0]):
        s_old = s_ref[d]
        so_ref[d] = s_old * gamma + k[d:d + 1, :] * v
        o = o + qg[d:d + 1, :] * s_old
    o_ref[...] = _group_norm(o, 0) * gn_ref[...] * _silu(g_ref[...])


def _retention_decode(q_t, k_t, v_t, g_t, state_t, ret_gn):
    width, b = q_t.shape
    hd = HEAD_DIM
    heads = width // hd
    gamma = jnp.exp(_log_gamma(heads))
    vec = pl.BlockSpec((hd, b), lambda h: (h, 0))
    st = pl.BlockSpec((None, None, hd, hd, b), lambda h: (0, h, 0, 0, 0))
    return pl.pallas_call(
        _ret_dec_kernel,
        out_shape=(jax.ShapeDtypeStruct((width, b), F32), jax.ShapeDtypeStruct(state_t.shape, F32)),
        grid=(heads,),
        in_specs=[pl.BlockSpec(memory_space=pltpu.SMEM), vec, vec, vec, vec, vec, st],
        out_specs=(vec, st),
        compiler_params=_cparams("arbitrary"),
        name="retention_decode",
    )(gamma, q_t, k_t, v_t, g_t, jnp.broadcast_to(ret_gn[:, None], (width, b)), state_t)


def _sbp_kernel(bias_ref, q_ref, k_ref, v_ref, tt_ref, o_ref, acc_scr, o_scr, *, heads):
    blk = SB_BLOCK
    qi = pl.program_id(1)
    acc_scr[...] = jnp.zeros_like(acc_scr)
    o_scr[...] = jnp.zeros_like(o_scr)
    q = q_ref[...] * (HEAD_DIM ** -0.5)
    q_heads = [q[:, h * HEAD_DIM:(h + 1) * HEAD_DIM].astype(BF16) for h in range(heads)]
    row = lax.broadcasted_iota(I32, (blk, blk), 0)
    col = lax.broadcasted_iota(I32, (blk, blk), 1)

    def key_block(j, masked):
        start = pl.multiple_of(j * blk, blk)
        kb = k_ref[:, pl.ds(start, blk)]
        vb = v_ref[:, pl.ds(start, blk)]
        valid = col < row
        betas, lposs = [], []
        for h in range(heads):
            sl = slice(h * HEAD_DIM, (h + 1) * HEAD_DIM)
            beta, lpos = _stick_terms(_dot(q_heads[h], kb[sl, :].astype(BF16)) + bias_ref[h])
            betas.append(beta)
            lposs.append(jnp.where(valid, lpos, 0.0) if masked else lpos)
        hi, lo = _split_bf16(jnp.concatenate(lposs, axis=0))
        tt = tt_ref[:, :blk]
        sums = _dot(hi, tt) + _dot(lo, tt)
        for h in range(heads):
            sl = slice(h * HEAD_DIM, (h + 1) * HEAD_DIM)
            rs = slice(h * blk, (h + 1) * blk)
            acc = acc_scr[h]
            w = betas[h] * jnp.exp(lposs[h] - sums[rs, :blk] - acc)
            if masked:
                w = jnp.where(valid, w, 0.0)
            o_scr[h] = o_scr[h] + _dot_nt(w.astype(BF16), vb[sl, :].astype(BF16))
            acc_scr[h] = acc + jnp.broadcast_to(sums[rs, :1], (blk, blk))

    key_block(qi, True)

    def body(jj, carry):
        key_block(qi - jj, False)
        return carry

    lax.fori_loop(1, qi + 1, body, 0)
    o_ref[...] = jnp.concatenate([o_scr[h] for h in range(heads)], axis=1)


def _sb_prompt(q, k_t, v_t, bias, seq):
    batch, width, _ = k_t.shape
    heads = width // HEAD_DIM
    blk = SB_BLOCK
    assert seq % blk == 0
    nq = seq // blk
    kv = pl.BlockSpec((None, width, seq), lambda b, i: (b, 0, 0))
    return pl.pallas_call(
        functools.partial(_sbp_kernel, heads=heads),
        out_shape=jax.ShapeDtypeStruct(q.shape, F32),
        grid=(batch, nq),
        in_specs=[
            pl.BlockSpec(memory_space=pltpu.SMEM),
            pl.BlockSpec((blk, width), lambda b, i: (b * nq + i, 0)),
            kv, kv,
            pl.BlockSpec((blk, 2 * blk), lambda b, i: (0, 0)),
        ],
        out_specs=pl.BlockSpec((blk, width), lambda b, i: (b * nq + i, 0)),
        scratch_shapes=[pltpu.VMEM((heads, blk, blk), F32), pltpu.VMEM((heads, blk, HEAD_DIM), F32)],
        compiler_params=_cparams("arbitrary", "arbitrary"),
        name="sb_prompt",
    )(bias, q, k_t, v_t, _suffix_matrix(blk))


def _sbd_kernel(pt_ref, q_ref, ks_ref, vs_ref, bias_ref, tt_ref, ck_ref, cv_ref, o_ref,
                k_buf, v_buf, sem, acc_scr, o_scr, *, heads, pages, page, n_pages, past_len):
    bi, s = pl.program_id(0), pl.program_id(1)
    n_steps = pl.num_programs(1)
    step = bi * n_steps + s
    slot = lax.rem(step, 2)
    width = heads * HEAD_DIM
    scale = HEAD_DIM ** -0.5
    q_pos = past_len

    def page_copies(row, sweep, buf_slot, i):
        phys = pt_ref[row, n_pages - 1 - (sweep * pages + i)]
        return (pltpu.make_async_copy(ck_ref.at[0, phys], k_buf.at[buf_slot, i], sem.at[buf_slot]),
                pltpu.make_async_copy(cv_ref.at[0, phys], v_buf.at[buf_slot, i], sem.at[buf_slot]))

    def start_fetch(row, sweep, buf_slot):
        for i in range(pages):
            for cp in page_copies(row, sweep, buf_slot, i):
                cp.start()

    @pl.when(step == 0)
    def _():
        start_fetch(0, 0, 0)

    @pl.when(step + 1 < pl.num_programs(0) * n_steps)
    def _():
        last = s + 1 == n_steps
        start_fetch(jnp.where(last, bi + 1, bi), jnp.where(last, 0, s + 1), 1 - slot)

    bd = _head_of_lane_mask(heads)
    q_bd = jnp.where(bd, q_ref[...], 0.0)
    bias = bias_ref[...]

    @pl.when(s == 0)
    def _():
        z = jnp.sum(q_bd * ks_ref[...], axis=1, keepdims=True) * scale + bias[:heads]
        valid = jnp.full(z.shape, past_len, I32) < q_pos
        beta, lpos = _stick_terms(z)
        lpos = jnp.where(valid, lpos, 0.0)
        w = jnp.where(valid, beta, 0.0)
        o_scr[...] = w * vs_ref[...]
        acc_scr[...] = jnp.broadcast_to(lpos, acc_scr.shape)

    for i in range(pages):
        for cp in page_copies(bi, s, slot, i):
            cp.wait()

    q_b = q_bd.astype(BF16)
    z = jnp.concatenate([_dot(q_b, k_buf[slot, i].reshape(width, page).astype(BF16)) for i in range(pages)],
                        axis=0) * scale + bias
    r = lax.broadcasted_iota(I32, z.shape, 0)
    lane = lax.broadcasted_iota(I32, z.shape, 1)
    logical_page = (n_pages - 1 - s * pages) - r // heads
    mask = (logical_page * page + lane) < q_pos
    beta, lpos = _stick_terms(z)
    lpos = jnp.where(mask, lpos, 0.0)
    hi, lo = _split_bf16(lpos)
    sums = _dot(hi, tt_ref[...]) + _dot(lo, tt_ref[...])
    carry = acc_scr[...]
    carries = []
    for i in range(pages):
        carries.append(carry)
        carry = carry + sums[i * heads:(i + 1) * heads, page:]
    acc_scr[...] = carry
    rev = sums[:, :page] + jnp.concatenate(carries, axis=0)
    w = jnp.where(mask, beta * jnp.exp(lpos - rev), 0.0)
    o = o_scr[...]
    for i in range(pages):
        o = o + _dot_nt(w[i * heads:(i + 1) * heads].astype(BF16),
                        v_buf[slot, i].reshape(width, page).astype(BF16))
    o_scr[...] = o

    @pl.when(s == n_steps - 1)
    def _():
        o_ref[...] = jnp.sum(jnp.where(bd, o_scr[...], 0.0), axis=0, keepdims=True)


def _sb_decode(q, k_self, v_self, bias, cache_k_t, cache_v_t, page_table):
    b, _, width = q.shape
    _, _, heads, hd, page = cache_k_t.shape
    n_pages = page_table.shape[1]
    pages = min(DEC_PAGES, n_pages)
    assert n_pages % pages == 0 and page == V7X_LANES and heads * hd == width
    vec = pl.BlockSpec((None, 1, width), lambda bi, s, pt: (bi, 0, 0))
    page_buf = pltpu.VMEM((2, pages, heads, hd, page), F32)
    grid_spec = pltpu.PrefetchScalarGridSpec(
        num_scalar_prefetch=1,
        grid=(b, n_pages // pages),
        in_specs=[
            vec, vec, vec,
            pl.BlockSpec((pages * heads, 1), lambda bi, s, pt: (0, 0)),
            pl.BlockSpec((page, 2 * page), lambda bi, s, pt: (0, 0)),
            pl.BlockSpec(memory_space=pl.ANY),
            pl.BlockSpec(memory_space=pl.ANY),
        ],
        out_specs=vec,
        scratch_shapes=[page_buf, page_buf, pltpu.SemaphoreType.DMA((2,)),
                        pltpu.VMEM((heads, page), F32), pltpu.VMEM((heads, width), F32)],
    )
    return pl.pallas_call(
        functools.partial(_sbd_kernel, heads=heads, pages=pages, page=page, n_pages=n_pages,
                          past_len=n_pages * page),
        out_shape=jax.ShapeDtypeStruct(q.shape, F32),
        grid_spec=grid_spec,
        compiler_params=_cparams("arbitrary", "arbitrary"),
        name="sb_decode",
    )(page_table, q, k_self, v_self, jnp.tile(bias, pages).reshape(pages * heads, 1), _suffix_matrix(page),
      cache_k_t, cache_v_t)


def _sb_both_kernel(pt_ref, bias_ref, q_ref, k_ref, v_ref, tt_ref, dq_ref, dks_ref, dvs_ref, dbias_ref,
                    ck_ref, cv_ref, o_ref, do_ref, acc_scr, o_scr, k_buf, v_buf, sem,
                    *, heads, pages, page, n_pages, past_len):
    blk = SB_BLOCK
    hd = HEAD_DIM
    width = heads * hd
    b, qi = pl.program_id(0), pl.program_id(1)
    nq = pl.num_programs(1)
    g = b * nq + qi
    n_rows = pl.num_programs(0) * nq
    n_sweeps = n_pages // pages
    q_pos = past_len

    def page_copies(row_idx, sweep, i):
        phys = pt_ref[row_idx, n_pages - 1 - (sweep * pages + i)]
        slot = sweep % 2
        return (pltpu.make_async_copy(ck_ref.at[0, phys], k_buf.at[slot, i], sem.at[slot]),
                pltpu.make_async_copy(cv_ref.at[0, phys], v_buf.at[slot, i], sem.at[slot]))

    def start_fetch(row_idx, sweep):
        for i in range(pages):
            for cp in page_copies(row_idx, sweep, i):
                cp.start()

    @pl.when(g == 0)
    def _():
        start_fetch(0, 0)

    acc_scr[...] = jnp.zeros_like(acc_scr)
    o_scr[...] = jnp.zeros_like(o_scr)
    q = q_ref[...] * (hd ** -0.5)
    q_heads = [q[:, h * hd:(h + 1) * hd].astype(BF16) for h in range(heads)]
    row = lax.broadcasted_iota(I32, (blk, blk), 0)
    col = lax.broadcasted_iota(I32, (blk, blk), 1)

    def key_block(j, masked):
        start = pl.multiple_of(j * blk, blk)
        kb = k_ref[:, pl.ds(start, blk)]
        vb = v_ref[:, pl.ds(start, blk)]
        valid = col < row
        betas, lposs = [], []
        for h in range(heads):
            sl = slice(h * hd, (h + 1) * hd)
            beta, lpos = _stick_terms(_dot(q_heads[h], kb[sl, :].astype(BF16)) + bias_ref[h])
            betas.append(beta)
            lposs.append(jnp.where(valid, lpos, 0.0) if masked else lpos)
        hi, lo = _split_bf16(jnp.concatenate(lposs, axis=0))
        tt = tt_ref[:, :blk]
        sums = _dot(hi, tt) + _dot(lo, tt)
        for h in range(heads):
            sl = slice(h * hd, (h + 1) * hd)
            rs = slice(h * blk, (h + 1) * blk)
            acc = acc_scr[h]
            w = betas[h] * jnp.exp(lposs[h] - sums[rs, :blk] - acc)
            if masked:
                w = jnp.where(valid, w, 0.0)
            o_scr[h] = o_scr[h] + _dot_nt(w.astype(BF16), vb[sl, :].astype(BF16))
            acc_scr[h] = acc + jnp.broadcast_to(sums[rs, :1], (blk, blk))

    def older_blocks(lo_jj, hi_jj):
        def body(jj, carry):
            key_block(qi - jj, False)
            return carry

        lax.fori_loop(lo_jj, hi_jj, body, 0)

    bd = _head_of_lane_mask(heads)
    q_bd = jnp.where(bd, dq_ref[...], 0.0)
    dbias = dbias_ref[...]
    z0 = jnp.sum(q_bd * dks_ref[...], axis=1, keepdims=True) * (hd ** -0.5) + dbias[:heads]
    valid0 = jnp.full(z0.shape, past_len, I32) < q_pos
    beta0, lpos0 = _stick_terms(z0)
    lpos0 = jnp.where(valid0, lpos0, 0.0)
    d_o = jnp.where(valid0, beta0, 0.0) * dvs_ref[...]
    d_acc = jnp.broadcast_to(lpos0, (heads, page))
    q_b = (q_bd * (hd ** -0.5)).astype(BF16)
    r = lax.broadcasted_iota(I32, (pages * heads, page), 0)
    lane = lax.broadcasted_iota(I32, (pages * heads, page), 1)

    key_block(qi, True)
    for s in range(n_sweeps):
        if s + 1 < n_sweeps:
            start_fetch(g, s + 1)
        else:
            @pl.when(g + 1 < n_rows)
            def _():
                start_fetch(g + 1, 0)

        older_blocks(1 + (s * qi) // n_sweeps, 1 + ((s + 1) * qi) // n_sweeps)

        for i in range(pages):
            for cp in page_copies(g, s, i):
                cp.wait()
        slot = s % 2
        z = jnp.concatenate([_dot(q_b, k_buf[slot, i].reshape(width, page).astype(BF16)) for i in range(pages)],
                            axis=0) + dbias
        logical_page = (n_pages - 1 - s * pages) - r // heads
        mask = (logical_page * page + lane) < q_pos
        beta, lpos = _stick_terms(z)
        lpos = jnp.where(mask, lpos, 0.0)
        hi, lo = _split_bf16(lpos)
        sums = _dot(hi, tt_ref[...]) + _dot(lo, tt_ref[...])
        carries = []
        for i in range(pages):
            carries.append(d_acc)
            d_acc = d_acc + sums[i * heads:(i + 1) * heads, page:]
        rev = sums[:, :page] + jnp.concatenate(carries, axis=0)
        w = jnp.where(mask, beta * jnp.exp(lpos - rev), 0.0)
        for i in range(pages):
            d_o = d_o + _dot_nt(w[i * heads:(i + 1) * heads].astype(BF16),
                                v_buf[slot, i].reshape(width, page).astype(BF16))

    o_ref[...] = jnp.concatenate([o_scr[h] for h in range(heads)], axis=1)
    do_ref[...] = jnp.sum(jnp.where(bd, d_o, 0.0), axis=0, keepdims=True)


def _sb_both(q, k_t, v_t, bias, seq, dq, dk_self, dv_self, cache_k_t, cache_v_t, page_table, pages):
    batch, width, _ = k_t.shape
    heads = width // HEAD_DIM
    blk = SB_BLOCK
    nq = seq // blk
    rows = dq.shape[0]
    _, _, _, hd, page = cache_k_t.shape
    n_pages = page_table.shape[1]
    assert rows == batch * nq and n_pages % pages == 0 and (n_pages // pages) % 2 == 0 and page == blk
    kv = pl.BlockSpec((None, width, seq), lambda b, i, pt: (b, 0, 0))
    qrow = pl.BlockSpec((blk, width), lambda b, i, pt: (b * nq + i, 0))
    vec = pl.BlockSpec((None, 1, width), lambda b, i, pt: (b * nq + i, 0, 0))
    page_buf = pltpu.VMEM((2, pages, heads, hd, page), F32)
    grid_spec = pltpu.PrefetchScalarGridSpec(
        num_scalar_prefetch=1,
        grid=(batch, nq),
        in_specs=[
            pl.BlockSpec(memory_space=pltpu.SMEM),
            qrow, kv, kv,
            pl.BlockSpec((blk, 2 * blk), lambda b, i, pt: (0, 0)),
            vec, vec, vec,
            pl.BlockSpec((pages * heads, 1), lambda b, i, pt: (0, 0)),
            pl.BlockSpec(memory_space=pl.ANY),
            pl.BlockSpec(memory_space=pl.ANY),
        ],
        out_specs=(qrow, vec),
        scratch_shapes=[pltpu.VMEM((heads, blk, blk), F32), pltpu.VMEM((heads, blk, HEAD_DIM), F32),
                        page_buf, page_buf, pltpu.SemaphoreType.DMA((2,))],
    )
    return pl.pallas_call(
        functools.partial(_sb_both_kernel, heads=heads, pages=pages, page=page, n_pages=n_pages,
                          past_len=n_pages * page),
        out_shape=(jax.ShapeDtypeStruct(q.shape, F32), jax.ShapeDtypeStruct(dq.shape, F32)),
        grid_spec=grid_spec,
        compiler_params=_cparams("arbitrary", "arbitrary"),
        name="sb_both",
    )(page_table, bias, q, k_t, v_t, _suffix_matrix(blk), dq, dk_self, dv_self,
      jnp.tile(bias, pages).reshape(pages * heads, 1), cache_k_t, cache_v_t)


def _out_kernel(or_ref, os_ref, x_ref, gt_ref, sh_ref, sc_ref, g_ref, w_ref, wrh_ref, wrl_ref, br_ref,
                x1_ref, h2_ref, lg_ref):
    half = or_ref.shape[1]
    mixed = _dot(or_ref[...].astype(BF16), w_ref[:half, :]) + _dot(os_ref[...].astype(BF16), w_ref[half:, :])
    x1 = x_ref[...] + gt_ref[...] * mixed
    x1_ref[...] = x1
    h2 = _rms_mod(x1, g_ref[...], sh_ref[...], sc_ref[...])
    h2_ref[...] = h2
    hh, hl = _split_bf16(h2)
    lg_ref[...] = (_dot_nt(wrh_ref[...], hh) + _dot_nt(wrh_ref[...], hl) + _dot_nt(wrl_ref[...], hh)
                   + br_ref[...])


def _out_proj(o_r, o_s, x, gate, shift, scale, g, w_bf16, w_router, b_router, *, tm, rows_per_mod):
    t_rows, d = x.shape
    half = o_r.shape[1]
    n_exp = w_router.shape[1]
    if rows_per_mod is None:
        mod_spec = pl.BlockSpec((tm, d), lambda i: (i, 0))
    else:
        tps = rows_per_mod // tm
        mod_spec = pl.BlockSpec((None, 1, d), lambda i: (i // tps, 0, 0))
    wr_hi, wr_lo = _split_bf16(w_router.T)
    const = lambda shape: pl.BlockSpec(shape, lambda i: (0, 0))
    return pl.pallas_call(
        _out_kernel,
        out_shape=(
            jax.ShapeDtypeStruct((t_rows, d), F32),
            jax.ShapeDtypeStruct((t_rows, d), F32),
            jax.ShapeDtypeStruct((n_exp, t_rows), F32),
        ),
        grid=(t_rows // tm,),
        in_specs=[
            pl.BlockSpec((tm, half), lambda i: (i, 0)),
            pl.BlockSpec((tm, half), lambda i: (i, 0)),
            pl.BlockSpec((tm, d), lambda i: (i, 0)),
            mod_spec, mod_spec, mod_spec,
            const((1, d)), const(w_bf16.shape), const((n_exp, d)), const((n_exp, d)), const((n_exp, 1)),
        ],
        out_specs=(
            pl.BlockSpec((tm, d), lambda i: (i, 0)),
            pl.BlockSpec((tm, d), lambda i: (i, 0)),
            pl.BlockSpec((n_exp, tm), lambda i: (0, i)),
        ),
        compiler_params=_cparams("arbitrary"),
        name="out_proj",
    )(o_r, o_s, x, gate, shift, scale, g, w_bf16, wr_hi, wr_lo, b_router.reshape(n_exp, 1))


def _route_kernel(lg_ref, tri_e_ref, upper_ref, dest_ref, gate_ref, be_ref, zf_ref, na_ref,
                  cnt_scr, base_scr, start_scr, *, tm):
    ph, i = pl.program_id(0), pl.program_id(1)
    logits = lg_ref[...]
    n_exp, tn = logits.shape
    e_iota = lax.broadcasted_iota(I32, (n_exp, tn), 0)
    work = logits
    sels, vals = [], []
    for _ in range(TOP_K):
        m = jnp.max(work, axis=0, keepdims=True)
        idx = jnp.min(jnp.where(work == m, e_iota, n_exp), axis=0, keepdims=True)
        sel = e_iota == idx
        sels.append(sel)
        vals.append(m)
        work = jnp.where(sel, -jnp.inf, work)
    chosen = sum(jnp.where(sel, 1.0, 0.0) for sel in sels)
    tile_count = jnp.sum(chosen, axis=1, keepdims=True)

    @pl.when(ph == 0)
    def _():
        @pl.when(i == 0)
        def _():
            cnt_scr[...] = jnp.zeros_like(cnt_scr)

        cnt_scr[...] = cnt_scr[...] + tile_count

    @pl.when((ph == 1) & (i == 0))
    def _():
        nblk = jnp.floor((cnt_scr[...] + (tm - 1)) * (1.0 / tm))
        first = _dot(tri_e_ref[...], nblk.astype(BF16))
        last = first + nblk
        start_scr[...] = first * tm
        base_scr[...] = jnp.zeros_like(base_scr)
        n_act = jnp.sum(nblk, axis=0, keepdims=True)
        na_ref[...] = n_act.astype(I32)
        for cb in range(be_ref.shape[1] // V7X_LANES):
            sl = slice(cb * V7X_LANES, (cb + 1) * V7X_LANES)
            bidx = (lax.broadcasted_iota(I32, (1, V7X_LANES), 1) + cb * V7X_LANES).astype(F32)
            owner = jnp.sum(jnp.where(last <= bidx, 1.0, 0.0), axis=0, keepdims=True)
            be_ref[:, sl] = jnp.minimum(owner, n_exp - 1.0).astype(I32)
            is_last = jnp.sum(jnp.where((last == bidx + 1.0) & (nblk > 0), 1.0, 0.0), axis=0, keepdims=True)
            zf_ref[:, sl] = jnp.where((is_last > 0) | (bidx >= n_act), 1, 0).astype(I32)

    @pl.when(ph == 1)
    def _():
        before = _dot(chosen.astype(BF16), upper_ref[...])
        row_of = start_scr[:, :1] + base_scr[:, :1] + before
        exps = [jnp.exp(v - vals[0]) for v in vals]
        denom = sum(exps)
        for kk in range(TOP_K):
            dest_ref[kk:kk + 1, :] = jnp.sum(jnp.where(sels[kk], row_of, 0.0), axis=0, keepdims=True).astype(I32)
            gate_ref[kk:kk + 1, :] = exps[kk] / denom
        base_scr[...] = base_scr[...] + tile_count


def _route(logits_t, tm, n_blocks):
    n_exp, t = logits_t.shape
    tn = max(c for c in (512, 384, 256, 128) if t % c == 0)
    nt = t // tn
    nb_pad = -(-n_blocks // V7X_LANES) * V7X_LANES
    assert t // tm + 1 < 256
    e = jnp.arange(n_exp)
    tri_e = (e[None, :] < e[:, None]).astype(BF16)
    tt = jnp.arange(tn)
    upper = (tt[:, None] < tt[None, :]).astype(BF16)
    tok = lambda ph, i: (0, i * ph)
    const = lambda ph, i: (0, 0)
    return pl.pallas_call(
        functools.partial(_route_kernel, tm=tm),
        out_shape=(
            jax.ShapeDtypeStruct((TOP_K, t), I32),
            jax.ShapeDtypeStruct((TOP_K, t), F32),
            jax.ShapeDtypeStruct((1, nb_pad), I32),
            jax.ShapeDtypeStruct((1, nb_pad), I32),
            jax.ShapeDtypeStruct((1, V7X_LANES), I32),
        ),
        grid=(2, nt),
        in_specs=[
            pl.BlockSpec((n_exp, tn), lambda ph, i: (0, i)),
            pl.BlockSpec((n_exp, n_exp), const),
            pl.BlockSpec((tn, tn), const),
        ],
        out_specs=(
            pl.BlockSpec((TOP_K, tn), tok),
            pl.BlockSpec((TOP_K, tn), tok),
            pl.BlockSpec((1, nb_pad), const),
            pl.BlockSpec((1, nb_pad), const),
            pl.BlockSpec((1, V7X_LANES), const),
        ),
        scratch_shapes=[pltpu.VMEM((n_exp, V7X_LANES), F32)] * 3,
        compiler_params=_cparams("arbitrary", "arbitrary"),
        name="route",
    )(logits_t, tri_e, upper)


def _for_each_token(groups, fn):
    for grp in range(groups):
        def body(lane, carry, grp=grp):
            fn(grp, lane, grp * V7X_LANES + lane)
            return carry

        lax.fori_loop(0, V7X_LANES, body, 0, unroll=2)


def _dispatch_kernel(zf_ref, dest_ref, h_ref, xb_ref, zero_scr, sem, zsem, *, tm, n_blocks):
    td = h_ref.shape[0]

    def zero_copy(b):
        return pltpu.make_async_copy(zero_scr, xb_ref.at[pl.ds(b * tm, tm)], zsem)

    @pl.when(pl.program_id(0) == 0)
    def _():
        zero_scr[...] = jnp.zeros_like(zero_scr)

        def start(b, c):
            @pl.when(zf_ref[b] > 0)
            def _():
                zero_copy(b).start()
            return c

        def wait(b, c):
            @pl.when(zf_ref[b] > 0)
            def _():
                zero_copy(b).wait()
            return c

        lax.fori_loop(0, n_blocks, start, 0)
        lax.fori_loop(0, n_blocks, wait, 0)

    def row_copy(t, dst_row):
        return pltpu.make_async_copy(h_ref.at[pl.ds(t, 1)], xb_ref.at[pl.ds(dst_row, 1)], sem)

    def start_tok(grp, lane, t):
        for kk in range(TOP_K):
            row_copy(t, dest_ref[grp, kk, lane]).start(priority=kk % 2)

    def wait_tok(grp, lane, t):
        for kk in range(TOP_K):
            row_copy(0, 0).wait()

    _for_each_token(td // V7X_LANES, start_tok)
    _for_each_token(td // V7X_LANES, wait_tok)


def _dispatch(h2, dest3, zero_flags, tm, n_blocks):
    t, d = h2.shape
    groups = dest3.shape[0]
    m = max(c for c in (4, 3, 2, 1) if groups % c == 0)
    td = m * V7X_LANES
    grid_spec = pltpu.PrefetchScalarGridSpec(
        num_scalar_prefetch=1,
        grid=(t // td,),
        in_specs=[
            pl.BlockSpec((m, TOP_K, V7X_LANES), lambda i, zf: (i, 0, 0), memory_space=pltpu.SMEM),
            pl.BlockSpec((td, d), lambda i, zf: (i, 0)),
        ],
        out_specs=pl.BlockSpec(memory_space=pl.ANY),
        scratch_shapes=[pltpu.VMEM((tm, d), F32), pltpu.SemaphoreType.DMA, pltpu.SemaphoreType.DMA],
    )
    return pl.pallas_call(
        functools.partial(_dispatch_kernel, tm=tm, n_blocks=n_blocks),
        out_shape=jax.ShapeDtypeStruct((n_blocks * tm, d), F32),
        grid_spec=grid_spec,
        compiler_params=_cparams("arbitrary"),
        name="dispatch",
    )(zero_flags, dest3, h2)


def _expert_kernel(be_ref, na_ref, x_ref, wgu_ref, bgu_ref, wdn_ref, bdn_ref, o_ref):
    i = pl.program_id(0)
    d_exp = wdn_ref.shape[0]

    @pl.when(i < na_ref[0])
    def _():
        h = _dot(x_ref[...].astype(BF16), wgu_ref[...].astype(BF16)) + bgu_ref[...]
        g = jnp.minimum(h[:, :d_exp], SWIGLU_LIMIT)
        u = jnp.clip(h[:, d_exp:], -SWIGLU_LIMIT, SWIGLU_LIMIT)
        a = g * jax.nn.sigmoid(SWIGLU_ALPHA * g) * (u + 1)
        o_ref[...] = _dot(a.astype(BF16), wdn_ref[...].astype(BF16)) + bdn_ref[...]

    @pl.when(i >= na_ref[0])
    def _():
        o_ref[...] = jnp.zeros_like(o_ref)


def _experts(xb, block_e, n_act, w_gu, b_gu, w_dn, b_dn, tm):
    n_rows, d = xb.shape
    _, n_exp, _, two_de = w_gu.shape
    d_exp = two_de // 2
    nb = n_rows // tm
    act = lambda i, na: jnp.minimum(i, na[0] - 1)
    grid_spec = pltpu.PrefetchScalarGridSpec(
        num_scalar_prefetch=2,
        grid=(nb,),
        in_specs=[
            pl.BlockSpec((tm, d), lambda i, be, na: (act(i, na), 0)),
            pl.BlockSpec((None, None, d, two_de), lambda i, be, na: (0, be[act(i, na)], 0, 0)),
            pl.BlockSpec((None, 1, two_de), lambda i, be, na: (be[act(i, na)], 0, 0)),
            pl.BlockSpec((None, None, d_exp, d), lambda i, be, na: (0, be[act(i, na)], 0, 0)),
            pl.BlockSpec((None, 1, d), lambda i, be, na: (be[act(i, na)], 0, 0)),
        ],
        out_specs=pl.BlockSpec((tm, d), lambda i, be, na: (i, 0)),
    )
    return pl.pallas_call(
        _expert_kernel,
        out_shape=jax.ShapeDtypeStruct((n_rows, d), F32),
        grid_spec=grid_spec,
        compiler_params=_cparams("arbitrary"),
        name="experts",
    )(block_e, n_act, xb, w_gu, b_gu.reshape(n_exp, 1, two_de), w_dn, b_dn.reshape(n_exp, 1, d))


def _final_kernel(dest_ref, x1_ref, gates_ref, gt_ref, sh_ref, sc_ref, g_ref, yb_ref, y_ref, rows_scr, sem):
    tf = x1_ref.shape[0]

    def row_copy(src_row, kk, t):
        return pltpu.make_async_copy(yb_ref.at[pl.ds(src_row, 1)], rows_scr.at[kk, pl.ds(t, 1)], sem)

    def start_tok(grp, lane, t):
        for kk in range(TOP_K):
            row_copy(dest_ref[grp, kk, lane], kk, t).start(priority=kk % 2)

    def wait_tok(grp, lane, t):
        for kk in range(TOP_K):
            row_copy(0, 0, 0).wait()

    _for_each_token(tf // V7X_LANES, start_tok)
    _for_each_token(tf // V7X_LANES, wait_tok)
    gates = gates_ref[...]
    moe = sum(gates[:, kk:kk + 1] * rows_scr[kk] for kk in range(TOP_K))
    x2 = x1_ref[...] + gt_ref[...] * moe
    y_ref[...] = _rms_mod(x2, g_ref[...], sh_ref[...], sc_ref[...])


def _final(x1, dest3, gates_t, gate2, shift, scale, g, yb, *, tok0, tf, rows_per_mod):
    t_rows, d = x1.shape
    m = tf // V7X_LANES
    assert tok0 % tf == 0
    first = tok0 // tf
    if rows_per_mod is None:
        mod_spec = pl.BlockSpec((tf, d), lambda i: (i, 0))
    else:
        tps = rows_per_mod // tf
        mod_spec = pl.BlockSpec((None, 1, d), lambda i: (i // tps, 0, 0))
    return pl.pallas_call(
        _final_kernel,
        out_shape=jax.ShapeDtypeStruct((t_rows, d), F32),
        grid=(t_rows // tf,),
        in_specs=[
            pl.BlockSpec((m, TOP_K, V7X_LANES), lambda i: (first + i, 0, 0), memory_space=pltpu.SMEM),
            pl.BlockSpec((tf, d), lambda i: (i, 0)),
            pl.BlockSpec((tf, TOP_K), lambda i: (first + i, 0)),
            mod_spec, mod_spec, mod_spec,
            pl.BlockSpec((1, d), lambda i: (0, 0)),
            pl.BlockSpec(memory_space=pl.ANY),
        ],
        out_specs=pl.BlockSpec((tf, d), lambda i: (i, 0)),
        scratch_shapes=[pltpu.VMEM((TOP_K, tf, d), F32), pltpu.SemaphoreType.DMA],
        compiler_params=_cparams("arbitrary"),
        name="final",
    )(dest3, x1, gates_t, gate2, shift, scale, g, yb)


def kernel(x_prompt, x_sample, cache_k, cache_v, state_ret, page_table, c_prompt, c_sample, w_ada, b_ada, g_mix, g_ffn, w_in, sb_bias, ret_gn, w_out, w_router, b_router, w_gate_up, b_gate_up, w_down, b_down, w_ada_final, b_ada_final, g_final):
    batch, seq, d = x_prompt.shape
    dec_b, dec_seq, _ = x_sample.shape
    depth = w_ada.shape[0]
    assert depth == 1 and dec_seq == 1
    page = cache_k.shape[2]
    ret_w = ret_gn.shape[1]
    sb_w = w_out.shape[1] - ret_w
    assert ret_w == sb_w
    sb_h = sb_w // HEAD_DIM
    n_exp = w_router.shape[2]
    tp, ts = batch * seq, dec_b
    tm_p = min(ROW_TILE, seq)
    past_len = page_table.shape[1] * page

    c_all = jnp.concatenate([c_prompt, c_sample], axis=0)
    mod = _ada(c_all, w_ada.reshape(w_ada.shape[1:]), b_ada[0])
    fin = _ada(c_all, w_ada_final, b_ada_final)
    mods_p = [mod[:batch, i * d:(i + 1) * d].reshape(batch, 1, d) for i in range(6)]
    mods_s = [mod[batch:, i * d:(i + 1) * d] for i in range(6)]
    fin_p = [fin[:batch, i * d:(i + 1) * d].reshape(batch, 1, d) for i in range(2)]
    fin_s = [fin[batch:, i * d:(i + 1) * d] for i in range(2)]

    w_in_b = w_in[0].astype(BF16)
    w_out_b = w_out[0].astype(BF16)
    g_mix2, g_ffn2, g_fin2 = g_mix.reshape(1, d), g_ffn.reshape(1, d), g_final.reshape(1, d)

    cos_p, sin_p = _rope_tables(jnp.arange(seq, dtype=F32))
    xp = x_prompt.reshape(tp, d)
    q_r, k_r, v_r, g_r, q_s, k_st, v_st = _proj(xp, mods_p[0], mods_p[1], g_mix2, w_in_b, cos_p, sin_p,
                                                tm=tm_p, rows_per_mod=seq)
    o_r, ret_p = _retention_prompt(q_r, k_r, v_r, g_r, ret_gn[0], batch, seq)

    cos_s, sin_s = _rope_tables(past_len + jnp.arange(1, dtype=F32))
    xs = x_sample.reshape(ts, d)
    sq_r, sk_r, sv_r, sg_r, sq_s, sk_st, sv_st = _proj(xs, mods_s[0], mods_s[1], g_mix2, w_in_b, cos_s, sin_s,
                                                       tm=ts, rows_per_mod=None)
    state_t = state_ret.transpose(0, 2, 3, 4, 1)
    so_rt, ret_st = _retention_decode(sq_r.T, sk_r.T, sv_r.T, sg_r.T, state_t, ret_gn[0])

    rows3 = lambda a: a.reshape(ts, 1, sb_w)
    dec_args = (rows3(sq_s), rows3(sk_st[0].T), rows3(sv_st[0].T))
    cache_k_t, cache_v_t = cache_k.transpose(0, 1, 3, 4, 2), cache_v.transpose(0, 1, 3, 4, 2)
    n_pages = page_table.shape[1]
    sweeps = n_pages // DEC_PAGES if n_pages % DEC_PAGES == 0 else 1
    if ts == batch * (seq // SB_BLOCK) and sweeps % 2 == 0:
        o_s, so_s = _sb_both(q_s, k_st, v_st, sb_bias[0], seq, *dec_args, cache_k_t, cache_v_t, page_table,
                             DEC_PAGES)
    else:
        o_s = _sb_prompt(q_s, k_st, v_st, sb_bias[0], seq)
        so_s = _sb_decode(*dec_args, sb_bias[0], cache_k_t, cache_v_t, page_table)

    x1_p, h2_p, lg_p = _out_proj(o_r, o_s, xp, mods_p[2], mods_p[3], mods_p[4], g_ffn2, w_out_b,
                                 w_router[0], b_router[0], tm=tm_p, rows_per_mod=seq)
    x1_s, h2_s, lg_s = _out_proj(so_rt.T, so_s.reshape(ts, sb_w), xs, mods_s[2], mods_s[3], mods_s[4],
                                 g_ffn2, w_out_b, w_router[0], b_router[0], tm=ts, rows_per_mod=None)

    tm = MOE_ROWS
    t_all = tp + ts
    assert t_all % V7X_LANES == 0 and tp % V7X_LANES == 0
    n_blocks = -(-(t_all * TOP_K) // tm) + n_exp
    h2_all = jnp.concatenate([h2_p, h2_s], axis=0)
    lg_all = jnp.concatenate([lg_p, lg_s], axis=1)
    dest, gates, block_e, zero_flags, n_act = _route(lg_all, tm, n_blocks)
    dest3 = dest.reshape(TOP_K, t_all // V7X_LANES, V7X_LANES).transpose(1, 0, 2)
    gates_t = gates.T
    xb = _dispatch(h2_all, dest3, zero_flags[0, :n_blocks], tm, n_blocks)
    yb = _experts(xb, block_e[0, :n_blocks], n_act[0, :1], w_gate_up, b_gate_up[0], w_down, b_down[0], tm)

    tf_p = min(256, seq)
    y_p = _final(x1_p, dest3, gates_t, mods_p[5], fin_p[0], fin_p[1], g_fin2, yb, tok0=0, tf=tf_p,
                 rows_per_mod=seq)
    y_s = _final(x1_s, dest3, gates_t, mods_s[5], fin_s[0], fin_s[1], g_fin2, yb, tok0=tp, tf=ts,
                 rows_per_mod=None)

    kv_rows = lambda a, b, l: a.reshape(1, b, sb_h, HEAD_DIM, l).transpose(0, 1, 4, 2, 3)
    return (
        y_p.reshape(batch, seq, d),
        y_s.reshape(dec_b, 1, d),
        kv_rows(k_st, batch, seq),
        kv_rows(v_st, batch, seq),
        ret_p[None],
        kv_rows(sk_st, 1, dec_b).transpose(0, 2, 1, 3, 4),
        kv_rows(sv_st, 1, dec_b).transpose(0, 2, 1, 3, 4),
        ret_st.transpose(0, 4, 1, 2, 3),
    )
```

```python
import functools

import jax
import jax.numpy as jnp
from jax import lax
from jax.experimental import pallas as pl
from jax.experimental.pallas import tpu as pltpu

F32, BF16, I32 = jnp.float32, jnp.bfloat16, jnp.int32

HEAD_DIM = 64
RET_CHUNK = 128
SB_BLOCK = 128
ROPE_BASE = 10000.0
TOP_K = 4
SWIGLU_LIMIT = 7.0
SWIGLU_ALPHA = 1.702
NORM_EPS = 1e-6

V7X_LANES = 128
V7X_VMEM_BYTES = 64 * 2**20
VMEM_LIMIT_BYTES = V7X_VMEM_BYTES - 8 * 2**20

ROW_TILE = 512
MOE_ROWS = 256
DEC_PAGES = 16
RET_STEP_CHUNKS = 2
N_ROW_GROUPS = 5


def _cparams(*sem):
    return pltpu.CompilerParams(dimension_semantics=sem, vmem_limit_bytes=VMEM_LIMIT_BYTES)


def _dot(a, b):
    return jnp.dot(a, b, preferred_element_type=F32)


def _dot_nt(a, b):
    return lax.dot_general(a, b, (((1,), (1,)), ((), ())), preferred_element_type=F32)


def _dot_tn(a, b):
    return lax.dot_general(a, b, (((0,), (0,)), ((), ())), preferred_element_type=F32)


def _split_bf16(x):
    hi = x.astype(BF16)
    return hi, (x - hi.astype(F32)).astype(BF16)


def _silu(x):
    return x * jax.nn.sigmoid(x)


def _rms_mod(x, g, shift, scale):
    y = x * lax.rsqrt(jnp.mean(x * x, axis=-1, keepdims=True) + NORM_EPS) * g
    return y * (1 + scale) + shift


def _group_norm(o, axis):
    mu = jnp.mean(o, axis=axis, keepdims=True)
    d = o - mu
    return d * lax.rsqrt(jnp.mean(d * d, axis=axis, keepdims=True) + NORM_EPS)


def _stick_terms(z):
    e = jnp.exp(-jnp.abs(z))
    p = 1.0 + e
    r = 1.0 / p
    beta = jnp.where(z >= 0, r, e * r)
    lpos = jnp.maximum(z, 0.0) + jnp.log(p)
    return beta, lpos


def _suffix_matrix(n):
    j = jnp.arange(n)
    return jnp.concatenate([(j[:, None] >= j[None, :]), jnp.ones((n, n), bool)], axis=1).astype(BF16)


def _head_of_lane_mask(heads):
    shape = (heads, heads * HEAD_DIM)
    return lax.broadcasted_iota(I32, shape, 0) == lax.broadcasted_iota(I32, shape, 1) // HEAD_DIM


def _ada_kernel(c_ref, w_ref, b_ref, o_ref):
    a = _silu(c_ref[...]).astype(BF16)
    o_ref[...] = _dot(a, w_ref[...].astype(BF16)) + b_ref[...]


def _ada(c, w, b, tn=512):
    m, d = c.shape
    n = w.shape[1]
    return pl.pallas_call(
        _ada_kernel,
        out_shape=jax.ShapeDtypeStruct((m, n), F32),
        grid=(n // tn,),
        in_specs=[
            pl.BlockSpec((m, d), lambda j: (0, 0)),
            pl.BlockSpec((d, tn), lambda j: (0, j)),
            pl.BlockSpec((1, tn), lambda j: (0, j)),
        ],
        out_specs=pl.BlockSpec((m, tn), lambda j: (0, j)),
        compiler_params=_cparams("arbitrary"),
        name="ada",
    )(c, w, b.reshape(1, n))


def _swap_halves(x):
    lane = lax.broadcasted_iota(I32, x.shape, 1)
    fwd = pltpu.roll(x, V7X_LANES - HEAD_DIM // 2, 1)
    bwd = pltpu.roll(x, HEAD_DIM // 2, 1)
    return jnp.where(lane % HEAD_DIM < HEAD_DIM // 2, fwd, bwd)


def _proj_kernel(x_ref, sh_ref, sc_ref, g_ref, w_ref, wt_ref, cos_ref, sin_ref, *out_refs, width):
    hb = _rms_mod(x_ref[...], g_ref[...], sh_ref[...], sc_ref[...]).astype(BF16)
    cos, sin = cos_ref[...], sin_ref[...]
    for gi, o_ref in enumerate(out_refs[:N_ROW_GROUPS]):
        acc = _dot(hb, w_ref[:, gi * width:(gi + 1) * width])
        if gi < 2:
            for c in range(width // V7X_LANES):
                sl = slice(c * V7X_LANES, (c + 1) * V7X_LANES)
                t = acc[:, sl]
                t = t * cos + _swap_halves(t) * sin
                o_ref[:, sl] = t * (HEAD_DIM ** -0.5) if gi == 1 else t
        else:
            o_ref[...] = acc
    for gi, o_ref in enumerate(out_refs[N_ROW_GROUPS:]):
        o_ref[...] = _dot_nt(wt_ref[gi * width:(gi + 1) * width, :], hb)


def _proj(x, shift, scale, g, w_bf16, cos, sin, *, tm, rows_per_mod):
    t_rows, d = x.shape
    width = w_bf16.shape[1] // 7
    w_row = w_bf16[:, :N_ROW_GROUPS * width]
    w_t = w_bf16[:, N_ROW_GROUPS * width:].T
    if rows_per_mod is None:
        tps, rows = 1, t_rows
        mod_spec = pl.BlockSpec((tm, d), lambda i: (i, 0))
        rope_spec = pl.BlockSpec((1, V7X_LANES), lambda i: (0, 0))
    else:
        tps, rows = rows_per_mod // tm, rows_per_mod
        mod_spec = pl.BlockSpec((None, 1, d), lambda i: (i // tps, 0, 0))
        rope_spec = pl.BlockSpec((tm, V7X_LANES), lambda i: (i % tps, 0))
    row_sds = jax.ShapeDtypeStruct((t_rows, width), F32)
    t_sds = jax.ShapeDtypeStruct((t_rows // rows, width, rows), F32)
    return pl.pallas_call(
        functools.partial(_proj_kernel, width=width),
        out_shape=(row_sds,) * N_ROW_GROUPS + (t_sds,) * 2,
        grid=(t_rows // tm,),
        in_specs=[
            pl.BlockSpec((tm, d), lambda i: (i, 0)),
            mod_spec,
            mod_spec,
            pl.BlockSpec((1, d), lambda i: (0, 0)),
            pl.BlockSpec(w_row.shape, lambda i: (0, 0)),
            pl.BlockSpec(w_t.shape, lambda i: (0, 0)),
            rope_spec,
            rope_spec,
        ],
        out_specs=(pl.BlockSpec((tm, width), lambda i: (i, 0)),) * N_ROW_GROUPS
        + (pl.BlockSpec((None, width, tm), lambda i: (i // tps, 0, i % tps)),) * 2,
        compiler_params=_cparams("arbitrary"),
        name="proj",
    )(x, shift, scale, g, w_row, w_t, cos, sin)


def _rope_tables(pos):
    half = HEAD_DIM // 2
    freq = ROPE_BASE ** (-jnp.arange(half, dtype=F32) / half)
    ang = pos[:, None] * freq[None, :]
    cos, sin = jnp.cos(ang), jnp.sin(ang)
    reps = V7X_LANES // HEAD_DIM
    return jnp.tile(cos, (1, 2 * reps)), jnp.tile(jnp.concatenate([-sin, sin], axis=1), (1, reps))


def _log_gamma(heads):
    return jnp.log1p(-jnp.exp2(-5.0 - jnp.arange(heads, dtype=F32)))


def _ret_kernel(cd_ref, q_ref, k_ref, v_ref, g_ref, gn_ref, intra_ref, qd_ref, kd_ref, o_ref, s_ref, *, heads):
    @pl.when(pl.program_id(1) == 0)
    def _():
        s_ref[...] = jnp.zeros_like(s_ref)

    c = RET_CHUNK
    for ci in range(q_ref.shape[0] // c):
        rows = slice(ci * c, (ci + 1) * c)
        q, k, v = q_ref[rows, :], k_ref[rows, :], v_ref[rows, :]
        qd = q * qd_ref[...]
        kd = k * kd_ref[...]
        outs = []
        for h in range(heads):
            sl = slice(h * HEAD_DIM, (h + 1) * HEAD_DIM)
            qh, kh, vh = q[:, sl].astype(BF16), k[:, sl].astype(BF16), v[:, sl].astype(BF16)
            s_old = s_ref[h]
            scores = _dot_nt(qh, kh) * intra_ref[h]
            o = _dot(scores.astype(BF16), vh) + _dot(qd[:, sl].astype(BF16), s_old.astype(BF16))
            s_ref[h] = s_old * cd_ref[h] + _dot_tn(kd[:, sl].astype(BF16), vh)
            outs.append(_group_norm(o, -1))
        o_ref[rows, :] = jnp.concatenate(outs, axis=1) * gn_ref[...] * _silu(g_ref[rows, :])


def _retention_prompt(q, k, v, g, ret_gn, batch, seq):
    width = q.shape[1]
    heads = width // HEAD_DIM
    c = RET_CHUNK
    per_step = RET_STEP_CHUNKS if seq % (RET_STEP_CHUNKS * c) == 0 else 1
    assert seq % c == 0
    nc = seq // (per_step * c)
    lg = _log_gamma(heads)
    t = jnp.arange(c, dtype=F32)
    diff = t[:, None] - t[None, :]
    intra = jnp.exp(jnp.where(diff[None] >= 0, diff[None] * lg[:, None, None], -jnp.inf))
    q_dec = jnp.repeat(jnp.exp((t[:, None] + 1.0) * lg[None, :]), HEAD_DIM, axis=1)
    k_dec = jnp.repeat(jnp.exp((c - 1.0 - t[:, None]) * lg[None, :]), HEAD_DIM, axis=1)
    chunk_dec = jnp.exp(c * lg)
    row = pl.BlockSpec((per_step * c, width), lambda b, i: (b * nc + i, 0))
    const2 = pl.BlockSpec((c, width), lambda b, i: (0, 0))
    return pl.pallas_call(
        functools.partial(_ret_kernel, heads=heads),
        out_shape=(
            jax.ShapeDtypeStruct((batch * seq, width), F32),
            jax.ShapeDtypeStruct((batch, heads, HEAD_DIM, HEAD_DIM), F32),
        ),
        grid=(batch, nc),
        in_specs=[
            pl.BlockSpec(memory_space=pltpu.SMEM),
            row, row, row, row,
            pl.BlockSpec((1, width), lambda b, i: (0, 0)),
            pl.BlockSpec((heads, c, c), lambda b, i: (0, 0, 0)),
            const2, const2,
        ],
        out_specs=(
            row,
            pl.BlockSpec((None, heads, HEAD_DIM, HEAD_DIM), lambda b, i: (b, 0, 0, 0)),
        ),
        compiler_params=_cparams("arbitrary", "arbitrary"),
        name="retention_prompt",
    )(chunk_dec, q, k, v, g, ret_gn.reshape(1, width), intra, q_dec, k_dec)


def _ret_dec_kernel(gam_ref, q_ref, k_ref, v_ref, g_ref, gn_ref, s_ref, o_ref, so_ref):
    gamma = gam_ref[pl.program_id(0)]
    q, k, v = q_ref[...], k_ref[...], v_ref[...]
    qg = q * gamma
    o = jnp.sum(q * k, axis=0, keepdims=True) * v
    for d in range(s_ref.shape[0]):
        s_old = s_ref[d]
        so_ref[d] = s_old * gamma + k[d:d + 1, :] * v
        o = o + qg[d:d + 1, :] * s_old
    o_ref[...] = _group_norm(o, 0) * gn_ref[...] * _silu(g_ref[...])


def _retention_decode(q_t, k_t, v_t, g_t, state_t, ret_gn):
    width, b = q_t.shape
    hd = HEAD_DIM
    heads = width // hd
    gamma = jnp.exp(_log_gamma(heads))
    vec = pl.BlockSpec((hd, b), lambda h: (h, 0))
    st = pl.BlockSpec((None, None, hd, hd, b), lambda h: (0, h, 0, 0, 0))
    return pl.pallas_call(
        _ret_dec_kernel,
        out_shape=(jax.ShapeDtypeStruct((width, b), F32), jax.ShapeDtypeStruct(state_t.shape, F32)),
        grid=(heads,),
        in_specs=[pl.BlockSpec(memory_space=pltpu.SMEM), vec, vec, vec, vec, vec, st],
        out_specs=(vec, st),
        compiler_params=_cparams("arbitrary"),
        name="retention_decode",
    )(gamma, q_t, k_t, v_t, g_t, jnp.broadcast_to(ret_gn[:, None], (width, b)), state_t)


def _sbp_kernel(bias_ref, q_ref, k_ref, v_ref, tt_ref, o_ref, acc_scr, o_scr, *, heads):
    blk = SB_BLOCK
    qi = pl.program_id(1)
    acc_scr[...] = jnp.zeros_like(acc_scr)
    o_scr[...] = jnp.zeros_like(o_scr)
    q = q_ref[...] * (HEAD_DIM ** -0.5)
    q_heads = [q[:, h * HEAD_DIM:(h + 1) * HEAD_DIM].astype(BF16) for h in range(heads)]
    row = lax.broadcasted_iota(I32, (blk, blk), 0)
    col = lax.broadcasted_iota(I32, (blk, blk), 1)

    def key_block(j, masked):
        start = pl.multiple_of(j * blk, blk)
        kb = k_ref[:, pl.ds(start, blk)]
        vb = v_ref[:, pl.ds(start, blk)]
        valid = col < row
        betas, lposs = [], []
        for h in range(heads):
            sl = slice(h * HEAD_DIM, (h + 1) * HEAD_DIM)
            beta, lpos = _stick_terms(_dot(q_heads[h], kb[sl, :].astype(BF16)) + bias_ref[h])
            betas.append(beta)
            lposs.append(jnp.where(valid, lpos, 0.0) if masked else lpos)
        hi, lo = _split_bf16(jnp.concatenate(lposs, axis=0))
        tt = tt_ref[:, :blk]
        sums = _dot(hi, tt) + _dot(lo, tt)
        for h in range(heads):
            sl = slice(h * HEAD_DIM, (h + 1) * HEAD_DIM)
            rs = slice(h * blk, (h + 1) * blk)
            acc = acc_scr[h]
            w = betas[h] * jnp.exp(lposs[h] - sums[rs, :blk] - acc)
            if masked:
                w = jnp.where(valid, w, 0.0)
            o_scr[h] = o_scr[h] + _dot_nt(w.astype(BF16), vb[sl, :].astype(BF16))
            acc_scr[h] = acc + jnp.broadcast_to(sums[rs, :1], (blk, blk))

    key_block(qi, True)

    def body(jj, carry):
        key_block(qi - jj, False)
        return carry

    lax.fori_loop(1, qi + 1, body, 0)
    o_ref[...] = jnp.concatenate([o_scr[h] for h in range(heads)], axis=1)


def _sb_prompt(q, k_t, v_t, bias, seq):
    batch, width, _ = k_t.shape
    heads = width // HEAD_DIM
    blk = SB_BLOCK
    assert seq % blk == 0
    nq = seq // blk
    kv = pl.BlockSpec((None, width, seq), lambda b, i: (b, 0, 0))
    return pl.pallas_call(
        functools.partial(_sbp_kernel, heads=heads),
        out_shape=jax.ShapeDtypeStruct(q.shape, F32),
        grid=(batch, nq),
        in_specs=[
            pl.BlockSpec(memory_space=pltpu.SMEM),
            pl.BlockSpec((blk, width), lambda b, i: (b * nq + i, 0)),
            kv, kv,
            pl.BlockSpec((blk, 2 * blk), lambda b, i: (0, 0)),
        ],
        out_specs=pl.BlockSpec((blk, width), lambda b, i: (b * nq + i, 0)),
        scratch_shapes=[pltpu.VMEM((heads, blk, blk), F32), pltpu.VMEM((heads, blk, HEAD_DIM), F32)],
        compiler_params=_cparams("arbitrary", "arbitrary"),
        name="sb_prompt",
    )(bias, q, k_t, v_t, _suffix_matrix(blk))


def _sbd_kernel(pt_ref, q_ref, ks_ref, vs_ref, bias_ref, tt_ref, ck_ref, cv_ref, o_ref,
                k_buf, v_buf, sem, acc_scr, o_scr, *, heads, pages, page, n_pages, past_len):
    bi, s = pl.program_id(0), pl.program_id(1)
    n_steps = pl.num_programs(1)
    step = bi * n_steps + s
    slot = lax.rem(step, 2)
    width = heads * HEAD_DIM
    scale = HEAD_DIM ** -0.5
    q_pos = past_len

    def page_copies(row, sweep, buf_slot, i):
        phys = pt_ref[row, n_pages - 1 - (sweep * pages + i)]
        return (pltpu.make_async_copy(ck_ref.at[0, phys], k_buf.at[buf_slot, i], sem.at[buf_slot]),
                pltpu.make_async_copy(cv_ref.at[0, phys], v_buf.at[buf_slot, i], sem.at[buf_slot]))

    def start_fetch(row, sweep, buf_slot):
        for i in range(pages):
            for cp in page_copies(row, sweep, buf_slot, i):
                cp.start()

    @pl.when(step == 0)
    def _():
        start_fetch(0, 0, 0)

    @pl.when(step + 1 < pl.num_programs(0) * n_steps)
    def _():
        last = s + 1 == n_steps
        start_fetch(jnp.where(last, bi + 1, bi), jnp.where(last, 0, s + 1), 1 - slot)

    bd = _head_of_lane_mask(heads)
    q_bd = jnp.where(bd, q_ref[...], 0.0)
    bias = bias_ref[...]

    @pl.when(s == 0)
    def _():
        z = jnp.sum(q_bd * ks_ref[...], axis=1, keepdims=True) * scale + bias[:heads]
        valid = jnp.full(z.shape, past_len, I32) < q_pos
        beta, lpos = _stick_terms(z)
        lpos = jnp.where(valid, lpos, 0.0)
        w = jnp.where(valid, beta, 0.0)
        o_scr[...] = w * vs_ref[...]
        acc_scr[...] = jnp.broadcast_to(lpos, acc_scr.shape)

    for i in range(pages):
        for cp in page_copies(bi, s, slot, i):
            cp.wait()

    q_b = q_bd.astype(BF16)
    z = jnp.concatenate([_dot(q_b, k_buf[slot, i].reshape(width, page).astype(BF16)) for i in range(pages)],
                        axis=0) * scale + bias
    r = lax.broadcasted_iota(I32, z.shape, 0)
    lane = lax.broadcasted_iota(I32, z.shape, 1)
    logical_page = (n_pages - 1 - s * pages) - r // heads
    mask = (logical_page * page + lane) < q_pos
    beta, lpos = _stick_terms(z)
    lpos = jnp.where(mask, lpos, 0.0)
    hi, lo = _split_bf16(lpos)
    sums = _dot(hi, tt_ref[...]) + _dot(lo, tt_ref[...])
    carry = acc_scr[...]
    carries = []
    for i in range(pages):
        carries.append(carry)
        carry = carry + sums[i * heads:(i + 1) * heads, page:]
    acc_scr[...] = carry
    rev = sums[:, :page] + jnp.concatenate(carries, axis=0)
    w = jnp.where(mask, beta * jnp.exp(lpos - rev), 0.0)
    o = o_scr[...]
    for i in range(pages):
        o = o + _dot_nt(w[i * heads:(i + 1) * heads].astype(BF16),
                        v_buf[slot, i].reshape(width, page).astype(BF16))
    o_scr[...] = o

    @pl.when(s == n_steps - 1)
    def _():
        o_ref[...] = jnp.sum(jnp.where(bd, o_scr[...], 0.0), axis=0, keepdims=True)


def _sb_decode(q, k_self, v_self, bias, cache_k_t, cache_v_t, page_table):
    b, _, width = q.shape
    _, _, heads, hd, page = cache_k_t.shape
    n_pages = page_table.shape[1]
    pages = min(DEC_PAGES, n_pages)
    assert n_pages % pages == 0 and page == V7X_LANES and heads * hd == width
    vec = pl.BlockSpec((None, 1, width), lambda bi, s, pt: (bi, 0, 0))
    page_buf = pltpu.VMEM((2, pages, heads, hd, page), F32)
    grid_spec = pltpu.PrefetchScalarGridSpec(
        num_scalar_prefetch=1,
        grid=(b, n_pages // pages),
        in_specs=[
            vec, vec, vec,
            pl.BlockSpec((pages * heads, 1), lambda bi, s, pt: (0, 0)),
            pl.BlockSpec((page, 2 * page), lambda bi, s, pt: (0, 0)),
            pl.BlockSpec(memory_space=pl.ANY),
            pl.BlockSpec(memory_space=pl.ANY),
        ],
        out_specs=vec,
        scratch_shapes=[page_buf, page_buf, pltpu.SemaphoreType.DMA((2,)),
                        pltpu.VMEM((heads, page), F32), pltpu.VMEM((heads, width), F32)],
    )
    return pl.pallas_call(
        functools.partial(_sbd_kernel, heads=heads, pages=pages, page=page, n_pages=n_pages,
                          past_len=n_pages * page),
        out_shape=jax.ShapeDtypeStruct(q.shape, F32),
        grid_spec=grid_spec,
        compiler_params=_cparams("arbitrary", "arbitrary"),
        name="sb_decode",
    )(page_table, q, k_self, v_self, jnp.tile(bias, pages).reshape(pages * heads, 1), _suffix_matrix(page),
      cache_k_t, cache_v_t)


def _sb_both_kernel(pt_ref, bias_ref, q_ref, k_ref, v_ref, tt_ref, dq_ref, dks_ref, dvs_ref, dbias_ref,
                    ck_ref, cv_ref, o_ref, do_ref, acc_scr, o_scr, k_buf, v_buf, sem,
                    *, heads, pages, page, n_pages, past_len):
    blk = SB_BLOCK
    hd = HEAD_DIM
    width = heads * hd
    b, qi = pl.program_id(0), pl.program_id(1)
    nq = pl.num_programs(1)
    g = b * nq + qi
    n_rows = pl.num_programs(0) * nq
    n_sweeps = n_pages // pages
    q_pos = past_len

    def page_copies(row_idx, sweep, i):
        phys = pt_ref[row_idx, n_pages - 1 - (sweep * pages + i)]
        slot = sweep % 2
        return (pltpu.make_async_copy(ck_ref.at[0, phys], k_buf.at[slot, i], sem.at[slot]),
                pltpu.make_async_copy(cv_ref.at[0, phys], v_buf.at[slot, i], sem.at[slot]))

    def start_fetch(row_idx, sweep):
        for i in range(pages):
            for cp in page_copies(row_idx, sweep, i):
                cp.start()

    @pl.when(g == 0)
    def _():
        start_fetch(0, 0)

    acc_scr[...] = jnp.zeros_like(acc_scr)
    o_scr[...] = jnp.zeros_like(o_scr)
    q = q_ref[...] * (hd ** -0.5)
    q_heads = [q[:, h * hd:(h + 1) * hd].astype(BF16) for h in range(heads)]
    row = lax.broadcasted_iota(I32, (blk, blk), 0)
    col = lax.broadcasted_iota(I32, (blk, blk), 1)

    def key_block(j, masked):
        start = pl.multiple_of(j * blk, blk)
        kb = k_ref[:, pl.ds(start, blk)]
        vb = v_ref[:, pl.ds(start, blk)]
        valid = col < row
        betas, lposs = [], []
        for h in range(heads):
            sl = slice(h * hd, (h + 1) * hd)
            beta, lpos = _stick_terms(_dot(q_heads[h], kb[sl, :].astype(BF16)) + bias_ref[h])
            betas.append(beta)
            lposs.append(jnp.where(valid, lpos, 0.0) if masked else lpos)
        hi, lo = _split_bf16(jnp.concatenate(lposs, axis=0))
        tt = tt_ref[:, :blk]
        sums = _dot(hi, tt) + _dot(lo, tt)
        for h in range(heads):
            sl = slice(h * hd, (h + 1) * hd)
            rs = slice(h * blk, (h + 1) * blk)
            acc = acc_scr[h]
            w = betas[h] * jnp.exp(lposs[h] - sums[rs, :blk] - acc)
            if masked:
                w = jnp.where(valid, w, 0.0)
            o_scr[h] = o_scr[h] + _dot_nt(w.astype(BF16), vb[sl, :].astype(BF16))
            acc_scr[h] = acc + jnp.broadcast_to(sums[rs, :1], (blk, blk))

    def older_blocks(lo_jj, hi_jj):
        def body(jj, carry):
            key_block(qi - jj, False)
            return carry

        lax.fori_loop(lo_jj, hi_jj, body, 0)

    bd = _head_of_lane_mask(heads)
    q_bd = jnp.where(bd, dq_ref[...], 0.0)
    dbias = dbias_ref[...]
    z0 = jnp.sum(q_bd * dks_ref[...], axis=1, keepdims=True) * (hd ** -0.5) + dbias[:heads]
    valid0 = jnp.full(z0.shape, past_len, I32) < q_pos
    beta0, lpos0 = _stick_terms(z0)
    lpos0 = jnp.where(valid0, lpos0, 0.0)
    d_o = jnp.where(valid0, beta0, 0.0) * dvs_ref[...]
    d_acc = jnp.broadcast_to(lpos0, (heads, page))
    q_b = (q_bd * (hd ** -0.5)).astype(BF16)
    r = lax.broadcasted_iota(I32, (pages * heads, page), 0)
    lane = lax.broadcasted_iota(I32, (pages * heads, page), 1)

    key_block(qi, True)
    for s in range(n_sweeps):
        if s + 1 < n_sweeps:
            start_fetch(g, s + 1)
        else:
            @pl.when(g + 1 < n_rows)
            def _():
                start_fetch(g + 1, 0)

        older_blocks(1 + (s * qi) // n_sweeps, 1 + ((s + 1) * qi) // n_sweeps)

        for i in range(pages):
            for cp in page_copies(g, s, i):
                cp.wait()
        slot = s % 2
        z = jnp.concatenate([_dot(q_b, k_buf[slot, i].reshape(width, page).astype(BF16)) for i in range(pages)],
                            axis=0) + dbias
        logical_page = (n_pages - 1 - s * pages) - r // heads
        mask = (logical_page * page + lane) < q_pos
        beta, lpos = _stick_terms(z)
        lpos = jnp.where(mask, lpos, 0.0)
        hi, lo = _split_bf16(lpos)
        sums = _dot(hi, tt_ref[...]) + _dot(lo, tt_ref[...])
        carries = []
        for i in range(pages):
            carries.append(d_acc)
            d_acc = d_acc + sums[i * heads:(i + 1) * heads, page:]
        rev = sums[:, :page] + jnp.concatenate(carries, axis=0)
        w = jnp.where(mask, beta * jnp.exp(lpos - rev), 0.0)
        for i in range(pages):
            d_o = d_o + _dot_nt(w[i * heads:(i + 1) * heads].astype(BF16),
                                v_buf[slot, i].reshape(width, page).astype(BF16))

    o_ref[...] = jnp.concatenate([o_scr[h] for h in range(heads)], axis=1)
    do_ref[...] = jnp.sum(jnp.where(bd, d_o, 0.0), axis=0, keepdims=True)


def _sb_both(q, k_t, v_t, bias, seq, dq, dk_self, dv_self, cache_k_t, cache_v_t, page_table, pages):
    batch, width, _ = k_t.shape
    heads = width // HEAD_DIM
    blk = SB_BLOCK
    nq = seq // blk
    rows = dq.shape[0]
    _, _, _, hd, page = cache_k_t.shape
    n_pages = page_table.shape[1]
    assert rows == batch * nq and n_pages % pages == 0 and (n_pages // pages) % 2 == 0 and page == blk
    kv = pl.BlockSpec((None, width, seq), lambda b, i, pt: (b, 0, 0))
    qrow = pl.BlockSpec((blk, width), lambda b, i, pt: (b * nq + i, 0))
    vec = pl.BlockSpec((None, 1, width), lambda b, i, pt: (b * nq + i, 0, 0))
    page_buf = pltpu.VMEM((2, pages, heads, hd, page), F32)
    grid_spec = pltpu.PrefetchScalarGridSpec(
        num_scalar_prefetch=1,
        grid=(batch, nq),
        in_specs=[
            pl.BlockSpec(memory_space=pltpu.SMEM),
            qrow, kv, kv,
            pl.BlockSpec((blk, 2 * blk), lambda b, i, pt: (0, 0)),
            vec, vec, vec,
            pl.BlockSpec((pages * heads, 1), lambda b, i, pt: (0, 0)),
            pl.BlockSpec(memory_space=pl.ANY),
            pl.BlockSpec(memory_space=pl.ANY),
        ],
        out_specs=(qrow, vec),
        scratch_shapes=[pltpu.VMEM((heads, blk, blk), F32), pltpu.VMEM((heads, blk, HEAD_DIM), F32),
                        page_buf, page_buf, pltpu.SemaphoreType.DMA((2,))],
    )
    return pl.pallas_call(
        functools.partial(_sb_both_kernel, heads=heads, pages=pages, page=page, n_pages=n_pages,
                          past_len=n_pages * page),
        out_shape=(jax.ShapeDtypeStruct(q.shape, F32), jax.ShapeDtypeStruct(dq.shape, F32)),
        grid_spec=grid_spec,
        compiler_params=_cparams("arbitrary", "arbitrary"),
        name="sb_both",
    )(page_table, bias, q, k_t, v_t, _suffix_matrix(blk), dq, dk_self, dv_self,
      jnp.tile(bias, pages).reshape(pages * heads, 1), cache_k_t, cache_v_t)


def _out_kernel(or_ref, os_ref, x_ref, gt_ref, sh_ref, sc_ref, g_ref, w_ref, wrh_ref, wrl_ref, br_ref,
                x1_ref, h2_ref, lg_ref):
    half = or_ref.shape[1]
    mixed = _dot(or_ref[...].astype(BF16), w_ref[:half, :]) + _dot(os_ref[...].astype(BF16), w_ref[half:, :])
    x1 = x_ref[...] + gt_ref[...] * mixed
    x1_ref[...] = x1
    h2 = _rms_mod(x1, g_ref[...], sh_ref[...], sc_ref[...])
    h2_ref[...] = h2
    hh, hl = _split_bf16(h2)
    lg_ref[...] = (_dot_nt(wrh_ref[...], hh) + _dot_nt(wrh_ref[...], hl) + _dot_nt(wrl_ref[...], hh)
                   + br_ref[...])


def _out_proj(o_r, o_s, x, gate, shift, scale, g, w_bf16, w_router, b_router, *, tm, rows_per_mod):
    t_rows, d = x.shape
    half = o_r.shape[1]
    n_exp = w_router.shape[1]
    if rows_per_mod is None:
        mod_spec = pl.BlockSpec((tm, d), lambda i: (i, 0))
    else:
        tps = rows_per_mod // tm
        mod_spec = pl.BlockSpec((None, 1, d), lambda i: (i // tps, 0, 0))
    wr_hi, wr_lo = _split_bf16(w_router.T)
    const = lambda shape: pl.BlockSpec(shape, lambda i: (0, 0))
    return pl.pallas_call(
        _out_kernel,
        out_shape=(
            jax.ShapeDtypeStruct((t_rows, d), F32),
            jax.ShapeDtypeStruct((t_rows, d), F32),
            jax.ShapeDtypeStruct((n_exp, t_rows), F32),
        ),
        grid=(t_rows // tm,),
        in_specs=[
            pl.BlockSpec((tm, half), lambda i: (i, 0)),
            pl.BlockSpec((tm, half), lambda i: (i, 0)),
            pl.BlockSpec((tm, d), lambda i: (i, 0)),
            mod_spec, mod_spec, mod_spec,
            const((1, d)), const(w_bf16.shape), const((n_exp, d)), const((n_exp, d)), const((n_exp, 1)),
        ],
        out_specs=(
            pl.BlockSpec((tm, d), lambda i: (i, 0)),
            pl.BlockSpec((tm, d), lambda i: (i, 0)),
            pl.BlockSpec((n_exp, tm), lambda i: (0, i)),
        ),
        compiler_params=_cparams("arbitrary"),
        name="out_proj",
    )(o_r, o_s, x, gate, shift, scale, g, w_bf16, wr_hi, wr_lo, b_router.reshape(n_exp, 1))


def _route_kernel(lg_ref, tri_e_ref, upper_ref, dest_ref, gate_ref, be_ref, zf_ref, na_ref,
                  cnt_scr, base_scr, start_scr, *, tm):
    ph, i = pl.program_id(0), pl.program_id(1)
    logits = lg_ref[...]
    n_exp, tn = logits.shape
    e_iota = lax.broadcasted_iota(I32, (n_exp, tn), 0)
    work = logits
    sels, vals = [], []
    for _ in range(TOP_K):
        m = jnp.max(work, axis=0, keepdims=True)
        idx = jnp.min(jnp.where(work == m, e_iota, n_exp), axis=0, keepdims=True)
        sel = e_iota == idx
        sels.append(sel)
        vals.append(m)
        work = jnp.where(sel, -jnp.inf, work)
    chosen = sum(jnp.where(sel, 1.0, 0.0) for sel in sels)
    tile_count = jnp.sum(chosen, axis=1, keepdims=True)

    @pl.when(ph == 0)
    def _():
        @pl.when(i == 0)
        def _():
            cnt_scr[...] = jnp.zeros_like(cnt_scr)

        cnt_scr[...] = cnt_scr[...] + tile_count

    @pl.when((ph == 1) & (i == 0))
    def _():
        nblk = jnp.floor((cnt_scr[...] + (tm - 1)) * (1.0 / tm))
        first = _dot(tri_e_ref[...], nblk.astype(BF16))
        last = first + nblk
        start_scr[...] = first * tm
        base_scr[...] = jnp.zeros_like(base_scr)
        n_act = jnp.sum(nblk, axis=0, keepdims=True)
        na_ref[...] = n_act.astype(I32)
        for cb in range(be_ref.shape[1] // V7X_LANES):
            sl = slice(cb * V7X_LANES, (cb + 1) * V7X_LANES)
            bidx = (lax.broadcasted_iota(I32, (1, V7X_LANES), 1) + cb * V7X_LANES).astype(F32)
            owner = jnp.sum(jnp.where(last <= bidx, 1.0, 0.0), axis=0, keepdims=True)
            be_ref[:, sl] = jnp.minimum(owner, n_exp - 1.0).astype(I32)
            is_last = jnp.sum(jnp.where((last == bidx + 1.0) & (nblk > 0), 1.0, 0.0), axis=0, keepdims=True)
            zf_ref[:, sl] = jnp.where((is_last > 0) | (bidx >= n_act), 1, 0).astype(I32)

    @pl.when(ph == 1)
    def _():
        before = _dot(chosen.astype(BF16), upper_ref[...])
        row_of = start_scr[:, :1] + base_scr[:, :1] + before
        exps = [jnp.exp(v - vals[0]) for v in vals]
        denom = sum(exps)
        for kk in range(TOP_K):
            dest_ref[kk:kk + 1, :] = jnp.sum(jnp.where(sels[kk], row_of, 0.0), axis=0, keepdims=True).astype(I32)
            gate_ref[kk:kk + 1, :] = exps[kk] / denom
        base_scr[...] = base_scr[...] + tile_count


def _route(logits_t, tm, n_blocks):
    n_exp, t = logits_t.shape
    tn = max(c for c in (512, 384, 256, 128) if t % c == 0)
    nt = t // tn
    nb_pad = -(-n_blocks // V7X_LANES) * V7X_LANES
    assert t // tm + 1 < 256
    e = jnp.arange(n_exp)
    tri_e = (e[None, :] < e[:, None]).astype(BF16)
    tt = jnp.arange(tn)
    upper = (tt[:, None] < tt[None, :]).astype(BF16)
    tok = lambda ph, i: (0, i * ph)
    const = lambda ph, i: (0, 0)
    return pl.pallas_call(
        functools.partial(_route_kernel, tm=tm),
        out_shape=(
            jax.ShapeDtypeStruct((TOP_K, t), I32),
            jax.ShapeDtypeStruct((TOP_K, t), F32),
            jax.ShapeDtypeStruct((1, nb_pad), I32),
            jax.ShapeDtypeStruct((1, nb_pad), I32),
            jax.ShapeDtypeStruct((1, V7X_LANES), I32),
        ),
        grid=(2, nt),
        in_specs=[
            pl.BlockSpec((n_exp, tn), lambda ph, i: (0, i)),
            pl.BlockSpec((n_exp, n_exp), const),
            pl.BlockSpec((tn, tn), const),
        ],
        out_specs=(
            pl.BlockSpec((TOP_K, tn), tok),
            pl.BlockSpec((TOP_K, tn), tok),
            pl.BlockSpec((1, nb_pad), const),
            pl.BlockSpec((1, nb_pad), const),
            pl.BlockSpec((1, V7X_LANES), const),
        ),
        scratch_shapes=[pltpu.VMEM((n_exp, V7X_LANES), F32)] * 3,
        compiler_params=_cparams("arbitrary", "arbitrary"),
        name="route",
    )(logits_t, tri_e, upper)


def _for_each_token(groups, fn):
    for grp in range(groups):
        def body(lane, carry, grp=grp):
            fn(grp, lane, grp * V7X_LANES + lane)
            return carry

        lax.fori_loop(0, V7X_LANES, body, 0, unroll=2)


def _dispatch_kernel(zf_ref, dest_ref, h_ref, xb_ref, zero_scr, sem, zsem, *, tm, n_blocks):
    td = h_ref.shape[0]

    def zero_copy(b):
        return pltpu.make_async_copy(zero_scr, xb_ref.at[pl.ds(b * tm, tm)], zsem)

    @pl.when(pl.program_id(0) == 0)
    def _():
        zero_scr[...] = jnp.zeros_like(zero_scr)

        def start(b, c):
            @pl.when(zf_ref[b] > 0)
            def _():
                zero_copy(b).start()
            return c

        def wait(b, c):
            @pl.when(zf_ref[b] > 0)
            def _():
                zero_copy(b).wait()
            return c

        lax.fori_loop(0, n_blocks, start, 0)
        lax.fori_loop(0, n_blocks, wait, 0)

    def row_copy(t, dst_row):
        return pltpu.make_async_copy(h_ref.at[pl.ds(t, 1)], xb_ref.at[dst_row], sem)

    def start_tok(grp, lane, t):
        for kk in range(TOP_K):
            row_copy(t, dest_ref[grp, kk, lane]).start(priority=kk % 2)

    def wait_tok(grp, lane, t):
        for kk in range(TOP_K):
            row_copy(0, 0).wait()

    _for_each_token(td // V7X_LANES, start_tok)
    _for_each_token(td // V7X_LANES, wait_tok)


def _dispatch(h2, dest3, zero_flags, tm, n_blocks):
    t, d = h2.shape
    groups = dest3.shape[0]
    m = max(c for c in (4, 3, 2, 1) if groups % c == 0)
    td = m * V7X_LANES
    grid_spec = pltpu.PrefetchScalarGridSpec(
        num_scalar_prefetch=1,
        grid=(t // td,),
        in_specs=[
            pl.BlockSpec((m, TOP_K, V7X_LANES), lambda i, zf: (i, 0, 0), memory_space=pltpu.SMEM),
            pl.BlockSpec((td, d), lambda i, zf: (i, 0)),
        ],
        out_specs=pl.BlockSpec(memory_space=pl.ANY),
        scratch_shapes=[pltpu.VMEM((tm, 1, d), F32), pltpu.SemaphoreType.DMA, pltpu.SemaphoreType.DMA],
    )
    return pl.pallas_call(
        functools.partial(_dispatch_kernel, tm=tm, n_blocks=n_blocks),
        out_shape=jax.ShapeDtypeStruct((n_blocks * tm, 1, d), F32),
        grid_spec=grid_spec,
        compiler_params=_cparams("arbitrary"),
        name="dispatch",
    )(zero_flags, dest3, h2)


def _expert_kernel(be_ref, na_ref, xb_ref, wgu_ref, bgu_ref, wdn_ref, bdn_ref, o_ref, x_buf, x_sem):
    i = pl.program_id(0)
    d_exp = wdn_ref.shape[0]
    tm = x_buf.shape[1]
    n_act = na_ref[0]

    def x_copy(blk, slot):
        return pltpu.make_async_copy(xb_ref.at[pl.ds(blk * tm, tm), 0], x_buf.at[slot], x_sem.at[slot])

    @pl.when(i == 0)
    def _():
        x_copy(0, 0).start()

    @pl.when(i + 1 < n_act)
    def _():
        x_copy(i + 1, lax.rem(i + 1, 2)).start()

    @pl.when(i < n_act)
    def _():
        slot = lax.rem(i, 2)
        x_copy(i, slot).wait()
        h = _dot(x_buf[slot].astype(BF16), wgu_ref[...].astype(BF16)) + bgu_ref[...]
        g = jnp.minimum(h[:, :d_exp], SWIGLU_LIMIT)
        u = jnp.clip(h[:, d_exp:], -SWIGLU_LIMIT, SWIGLU_LIMIT)
        a = g * jax.nn.sigmoid(SWIGLU_ALPHA * g) * (u + 1)
        o_ref[:, 0, :] = _dot(a.astype(BF16), wdn_ref[...].astype(BF16)) + bdn_ref[...]

    @pl.when(i >= na_ref[0])
    def _():
        o_ref[...] = jnp.zeros_like(o_ref)


def _experts(xb, block_e, n_act, w_gu, b_gu, w_dn, b_dn, tm):
    n_rows, _, d = xb.shape
    _, n_exp, _, two_de = w_gu.shape
    d_exp = two_de // 2
    nb = n_rows // tm
    act = lambda i, na: jnp.minimum(i, na[0] - 1)
    grid_spec = pltpu.PrefetchScalarGridSpec(
        num_scalar_prefetch=2,
        grid=(nb,),
        in_specs=[
            pl.BlockSpec(memory_space=pl.ANY),
            pl.BlockSpec((None, None, d, two_de), lambda i, be, na: (0, be[act(i, na)], 0, 0)),
            pl.BlockSpec((None, 1, two_de), lambda i, be, na: (be[act(i, na)], 0, 0)),
            pl.BlockSpec((None, None, d_exp, d), lambda i, be, na: (0, be[act(i, na)], 0, 0)),
            pl.BlockSpec((None, 1, d), lambda i, be, na: (be[act(i, na)], 0, 0)),
        ],
        out_specs=pl.BlockSpec((tm, 1, d), lambda i, be, na: (i, 0, 0)),
        scratch_shapes=[pltpu.VMEM((2, tm, d), F32), pltpu.SemaphoreType.DMA((2,))],
    )
    return pl.pallas_call(
        _expert_kernel,
        out_shape=jax.ShapeDtypeStruct((n_rows, 1, d), F32),
        grid_spec=grid_spec,
        compiler_params=_cparams("arbitrary"),
        name="experts",
    )(block_e, n_act, xb, w_gu, b_gu.reshape(n_exp, 1, two_de), w_dn, b_dn.reshape(n_exp, 1, d))


def _final_kernel(dest_ref, x1_ref, gates_ref, gt_ref, sh_ref, sc_ref, g_ref, yb_ref, y_ref, rows_scr, sem):
    tf = x1_ref.shape[0]

    def row_copy(src_row, kk, t):
        return pltpu.make_async_copy(yb_ref.at[src_row], rows_scr.at[kk, pl.ds(t, 1)], sem)

    def start_tok(grp, lane, t):
        for kk in range(TOP_K):
            row_copy(dest_ref[grp, kk, lane], kk, t).start(priority=kk % 2)

    def wait_tok(grp, lane, t):
        for kk in range(TOP_K):
            row_copy(0, 0, 0).wait()

    _for_each_token(tf // V7X_LANES, start_tok)
    _for_each_token(tf // V7X_LANES, wait_tok)
    gates = gates_ref[...]
    moe = sum(gates[:, kk:kk + 1] * rows_scr[kk] for kk in range(TOP_K))
    x2 = x1_ref[...] + gt_ref[...] * moe
    y_ref[...] = _rms_mod(x2, g_ref[...], sh_ref[...], sc_ref[...])


def _final(x1, dest3, gates_t, gate2, shift, scale, g, yb, *, tok0, tf, rows_per_mod):
    t_rows, d = x1.shape
    m = tf // V7X_LANES
    assert tok0 % tf == 0
    first = tok0 // tf
    if rows_per_mod is None:
        mod_spec = pl.BlockSpec((tf, d), lambda i: (i, 0))
    else:
        tps = rows_per_mod // tf
        mod_spec = pl.BlockSpec((None, 1, d), lambda i: (i // tps, 0, 0))
    return pl.pallas_call(
        _final_kernel,
        out_shape=jax.ShapeDtypeStruct((t_rows, d), F32),
        grid=(t_rows // tf,),
        in_specs=[
            pl.BlockSpec((m, TOP_K, V7X_LANES), lambda i: (first + i, 0, 0), memory_space=pltpu.SMEM),
            pl.BlockSpec((tf, d), lambda i: (i, 0)),
            pl.BlockSpec((tf, TOP_K), lambda i: (first + i, 0)),
            mod_spec, mod_spec, mod_spec,
            pl.BlockSpec((1, d), lambda i: (0, 0)),
            pl.BlockSpec(memory_space=pl.ANY),
        ],
        out_specs=pl.BlockSpec((tf, d), lambda i: (i, 0)),
        scratch_shapes=[pltpu.VMEM((TOP_K, tf, d), F32), pltpu.SemaphoreType.DMA],
        compiler_params=_cparams("arbitrary"),
        name="final",
    )(dest3, x1, gates_t, gate2, shift, scale, g, yb)


def kernel(x_prompt, x_sample, cache_k, cache_v, state_ret, page_table, c_prompt, c_sample, w_ada, b_ada, g_mix, g_ffn, w_in, sb_bias, ret_gn, w_out, w_router, b_router, w_gate_up, b_gate_up, w_down, b_down, w_ada_final, b_ada_final, g_final):
    batch, seq, d = x_prompt.shape
    dec_b, dec_seq, _ = x_sample.shape
    depth = w_ada.shape[0]
    assert depth == 1 and dec_seq == 1
    page = cache_k.shape[2]
    ret_w = ret_gn.shape[1]
    sb_w = w_out.shape[1] - ret_w
    assert ret_w == sb_w
    sb_h = sb_w // HEAD_DIM
    n_exp = w_router.shape[2]
    tp, ts = batch * seq, dec_b
    tm_p = min(ROW_TILE, seq)
    past_len = page_table.shape[1] * page

    c_all = jnp.concatenate([c_prompt, c_sample], axis=0)
    mod = _ada(c_all, w_ada.reshape(w_ada.shape[1:]), b_ada[0])
    fin = _ada(c_all, w_ada_final, b_ada_final)
    mods_p = [mod[:batch, i * d:(i + 1) * d].reshape(batch, 1, d) for i in range(6)]
    mods_s = [mod[batch:, i * d:(i + 1) * d] for i in range(6)]
    fin_p = [fin[:batch, i * d:(i + 1) * d].reshape(batch, 1, d) for i in range(2)]
    fin_s = [fin[batch:, i * d:(i + 1) * d] for i in range(2)]

    w_in_b = w_in[0].astype(BF16)
    w_out_b = w_out[0].astype(BF16)
    g_mix2, g_ffn2, g_fin2 = g_mix.reshape(1, d), g_ffn.reshape(1, d), g_final.reshape(1, d)

    cos_p, sin_p = _rope_tables(jnp.arange(seq, dtype=F32))
    xp = x_prompt.reshape(tp, d)
    q_r, k_r, v_r, g_r, q_s, k_st, v_st = _proj(xp, mods_p[0], mods_p[1], g_mix2, w_in_b, cos_p, sin_p,
                                                tm=tm_p, rows_per_mod=seq)
    o_r, ret_p = _retention_prompt(q_r, k_r, v_r, g_r, ret_gn[0], batch, seq)

    cos_s, sin_s = _rope_tables(past_len + jnp.arange(1, dtype=F32))
    xs = x_sample.reshape(ts, d)
    sq_r, sk_r, sv_r, sg_r, sq_s, sk_st, sv_st = _proj(xs, mods_s[0], mods_s[1], g_mix2, w_in_b, cos_s, sin_s,
                                                       tm=ts, rows_per_mod=None)
    state_t = state_ret.transpose(0, 2, 3, 4, 1)
    so_rt, ret_st = _retention_decode(sq_r.T, sk_r.T, sv_r.T, sg_r.T, state_t, ret_gn[0])

    rows3 = lambda a: a.reshape(ts, 1, sb_w)
    dec_args = (rows3(sq_s), rows3(sk_st[0].T), rows3(sv_st[0].T))
    cache_k_t, cache_v_t = cache_k.transpose(0, 1, 3, 4, 2), cache_v.transpose(0, 1, 3, 4, 2)
    n_pages = page_table.shape[1]
    sweeps = n_pages // DEC_PAGES if n_pages % DEC_PAGES == 0 else 1
    if ts == batch * (seq // SB_BLOCK) and sweeps % 2 == 0:
        o_s, so_s = _sb_both(q_s, k_st, v_st, sb_bias[0], seq, *dec_args, cache_k_t, cache_v_t, page_table,
                             DEC_PAGES)
    else:
        o_s = _sb_prompt(q_s, k_st, v_st, sb_bias[0], seq)
        so_s = _sb_decode(*dec_args, sb_bias[0], cache_k_t, cache_v_t, page_table)

    x1_p, h2_p, lg_p = _out_proj(o_r, o_s, xp, mods_p[2], mods_p[3], mods_p[4], g_ffn2, w_out_b,
                                 w_router[0], b_router[0], tm=tm_p, rows_per_mod=seq)
    x1_s, h2_s, lg_s = _out_proj(so_rt.T, so_s.reshape(ts, sb_w), xs, mods_s[2], mods_s[3], mods_s[4],
                                 g_ffn2, w_out_b, w_router[0], b_router[0], tm=ts, rows_per_mod=None)

    tm = MOE_ROWS
    t_all = tp + ts
    assert t_all % V7X_LANES == 0 and tp % V7X_LANES == 0
    n_blocks = -(-(t_all * TOP_K) // tm) + n_exp
    h2_all = jnp.concatenate([h2_p, h2_s], axis=0)
    lg_all = jnp.concatenate([lg_p, lg_s], axis=1)
    dest, gates, block_e, zero_flags, n_act = _route(lg_all, tm, n_blocks)
    dest3 = dest.reshape(TOP_K, t_all // V7X_LANES, V7X_LANES).transpose(1, 0, 2)
    gates_t = gates.T
    xb = _dispatch(h2_all, dest3, zero_flags[0, :n_blocks], tm, n_blocks)
    yb = _experts(xb, block_e[0, :n_blocks], n_act[0, :1], w_gate_up, b_gate_up[0], w_down, b_down[0], tm)

    tf_p = min(256, seq)
    y_p = _final(x1_p, dest3, gates_t, mods_p[5], fin_p[0], fin_p[1], g_fin2, yb, tok0=0, tf=tf_p,
                 rows_per_mod=seq)
    y_s = _final(x1_s, dest3, gates_t, mods_s[5], fin_s[0], fin_s[1], g_fin2, yb, tok0=tp, tf=ts,
                 rows_per_mod=None)

    kv_rows = lambda a, b, l: a.reshape(1, b, sb_h, HEAD_DIM, l).transpose(0, 1, 4, 2, 3)
    return (
        y_p.reshape(batch, seq, d),
        y_s.reshape(dec_b, 1, d),
        kv_rows(k_st, batch, seq),
        kv_rows(v_st, batch, seq),
        ret_p[None],
        kv_rows(sk_st, 1, dec_b).transpose(0, 2, 1, 3, 4),
        kv_rows(sv_st, 1, dec_b).transpose(0, 2, 1, 3, 4),
        ret_st.transpose(0, 4, 1, 2, 3),
    )
```
